```python
import jax, jax.numpy as jnp
from jax import lax
import numpy as np

D_MODEL = 4096
BATCH = 1
SEQ = 16384
DEPTH = 1

MIX_DIM = D_MODEL
RET_DIM = MIX_DIM // 2
POOL_DIM = MIX_DIM - RET_DIM
RET_HEADS = 8
RET_HEAD_DIM = RET_DIM // RET_HEADS
POOL_WINDOWS = (2, 4, 8, 16)
POOL_GROUPS = len(POOL_WINDOWS)
POOL_GROUP_DIM = POOL_DIM // POOL_GROUPS
PROJ_DIM = 4 * RET_DIM + POOL_DIM
D_FF = ((8 * D_MODEL // 3 + 255) // 256) * 256
CHUNK = 128
ROPE_BASE = 10000.0
EPS = 1e-6

kernel_name = 'hybrid_retention_pool_macaron_encoder'


def _rmsnorm(x, gain):
    xf = x.astype(jnp.float32)
    y = xf * lax.rsqrt(jnp.mean(xf * xf, axis=-1, keepdims=True) + EPS)
    return (y * gain.astype(jnp.float32)).astype(x.dtype)


def _swiglu(h, w_gate, w_up, w_down):
    return (jax.nn.silu(h @ w_gate) * (h @ w_up)) @ w_down


def _rotary(t, positions):
    d = t.shape[-1]
    freqs = 1.0 / (ROPE_BASE ** (jnp.arange(0, d, 2, dtype=jnp.float32) / d))
    ang = positions.astype(jnp.float32)[..., None] * freqs
    cos = jnp.cos(ang)[:, :, None, :]
    sin = jnp.sin(ang)[:, :, None, :]
    tf = t.astype(jnp.float32)
    t1, t2 = tf[..., : d // 2], tf[..., d // 2:]
    return jnp.concatenate([t1 * cos - t2 * sin, t1 * sin + t2 * cos], axis=-1)


def _retention_one_direction(q, k, v, log_gamma, include_diag):
    b, h, s, d = q.shape
    n_chunks = s // CHUNK

    def to_chunks(t):
        return t.reshape(b, h, n_chunks, CHUNK, t.shape[-1]).transpose(2, 0, 1, 3, 4)

    idx = jnp.arange(CHUNK, dtype=jnp.float32)
    rel = idx[:, None] - idx[None, :]
    mask = (rel >= 0) if include_diag else (rel > 0)
    lg = log_gamma[:, None, None]
    intra_decay = jnp.where(mask[None], jnp.exp(jnp.where(mask[None], rel[None], 0.0) * lg), 0.0)
    q_decay = jnp.exp((idx + 1.0)[None, :] * log_gamma[:, None])[..., None]
    k_decay = jnp.exp((CHUNK - 1.0 - idx)[None, :] * log_gamma[:, None])[..., None]
    chunk_decay = jnp.exp(CHUNK * log_gamma)[:, None, None]

    def step(state, qkv):
        qc, kc, vc = qkv
        scores = jnp.einsum('bhid,bhjd->bhij', qc, kc) * intra_decay
        inner = jnp.einsum('bhij,bhje->bhie', scores, vc)
        cross = jnp.einsum('bhid,bhde->bhie', qc * q_decay, state)
        state = state * chunk_decay + jnp.einsum('bhjd,bhje->bhde', kc * k_decay, vc)
        return state, inner + cross

    state0 = jnp.zeros((b, h, d, v.shape[-1]), jnp.float32)
    _, out = lax.scan(step, state0, (to_chunks(q), to_chunks(k), to_chunks(v)))
    return out.transpose(1, 2, 0, 3, 4).reshape(b, h, s, v.shape[-1])


def _bidirectional_retention(q, k, v, logit_fwd, logit_bwd):
    lg_f = jax.nn.log_sigmoid(logit_fwd.astype(jnp.float32))
    lg_b = jax.nn.log_sigmoid(logit_bwd.astype(jnp.float32))
    out_f = _retention_one_direction(q, k, v, lg_f, True)
    flip = lambda t: t[:, :, ::-1]
    out_b = flip(_retention_one_direction(flip(q), flip(k), flip(v), lg_b, False))
    return out_f + out_b


def _centred_mean_minus_self(p, window):
    b, s, c = p.shape
    cs = jnp.concatenate([jnp.zeros((b, 1, c), p.dtype), jnp.cumsum(p, axis=1)], axis=1)
    lo = window // 2
    hi = window - 1 - lo
    n = jnp.arange(s)
    start = jnp.clip(n - lo, 0, s)
    end = jnp.clip(n + hi + 1, 0, s)
    total = cs[:, end] - cs[:, start]
    count = (end - start).astype(p.dtype)
    return total / count[None, :, None] - p


def setup_inputs(seed: int = 0) -> dict:
    key = jax.random.key(seed)
    ks = jax.random.split(key, 24)
    f32 = jnp.float32

    def nrm(k, shape, fan_in):
        return jax.random.normal(k, shape, f32) * (fan_in ** -0.5)

    def gain(k, shape):
        return 1.0 + 0.02 * jax.random.normal(k, shape, f32)

    hh = jnp.arange(RET_HEADS, dtype=f32)
    gamma = 1.0 - jnp.exp2(-5.0 - hh)
    base_logit = jnp.log(gamma) - jnp.log1p(-gamma)

    x = jax.random.normal(ks[0], (BATCH, SEQ, D_MODEL), f32)
    positions = jnp.broadcast_to(jnp.arange(SEQ, dtype=jnp.int32)[None, :], (BATCH, SEQ))
    return {
        'x': x,
        'positions': positions,
        'ffn1_norm': gain(ks[1], (DEPTH, D_MODEL)),
        'ffn1_w_gate': nrm(ks[2], (DEPTH, D_MODEL, D_FF), D_MODEL),
        'ffn1_w_up': nrm(ks[3], (DEPTH, D_MODEL, D_FF), D_MODEL),
        'ffn1_w_down': nrm(ks[4], (DEPTH, D_FF, D_MODEL), D_FF),
        'mix_norm': gain(ks[5], (DEPTH, D_MODEL)),
        'w_in': nrm(ks[6], (DEPTH, D_MODEL, PROJ_DIM), D_MODEL),
        'ret_decay_fwd': base_logit[None] + 0.1 * jax.random.normal(ks[7], (DEPTH, RET_HEADS), f32),
        'ret_decay_bwd': base_logit[None] + 0.1 * jax.random.normal(ks[8], (DEPTH, RET_HEADS), f32),
        'ret_head_norm': gain(ks[9], (DEPTH, RET_DIM)),
        'pool_w': nrm(ks[10], (DEPTH, POOL_GROUPS, POOL_GROUP_DIM, POOL_GROUP_DIM), POOL_GROUP_DIM),
        'pool_scale': gain(ks[11], (DEPTH, POOL_DIM)),
        'w_out': nrm(ks[12], (DEPTH, MIX_DIM, D_MODEL), MIX_DIM),
        'ffn2_norm': gain(ks[13], (DEPTH, D_MODEL)),
        'ffn2_w_gate': nrm(ks[14], (DEPTH, D_MODEL, D_FF), D_MODEL),
        'ffn2_w_up': nrm(ks[15], (DEPTH, D_MODEL, D_FF), D_MODEL),
        'ffn2_w_down': nrm(ks[16], (DEPTH, D_FF, D_MODEL), D_FF),
        'final_norm': gain(ks[17], (D_MODEL,)),
    }


def reference(x, positions, ffn1_norm, ffn1_w_gate, ffn1_w_up, ffn1_w_down, mix_norm, w_in,
              ret_decay_fwd, ret_decay_bwd, ret_head_norm, pool_w, pool_scale, w_out,
              ffn2_norm, ffn2_w_gate, ffn2_w_up, ffn2_w_down, final_norm):
    b, s, _ = x.shape
    for layer in range(DEPTH):
        x = x + 0.5 * _swiglu(_rmsnorm(x, ffn1_norm[layer]), ffn1_w_gate[layer],
                              ffn1_w_up[layer], ffn1_w_down[layer])

        h = _rmsnorm(x, mix_norm[layer])
        z = h @ w_in[layer]
        q, k, v, g, p = jnp.split(z, [RET_DIM, 2 * RET_DIM, 3 * RET_DIM, 4 * RET_DIM], axis=-1)

        q = _rotary(q.reshape(b, s, RET_HEADS, RET_HEAD_DIM), positions)
        k = _rotary(k.reshape(b, s, RET_HEADS, RET_HEAD_DIM), positions) * (RET_HEAD_DIM ** -0.5)
        v = v.reshape(b, s, RET_HEADS, RET_HEAD_DIM).astype(jnp.float32)
        ret = _bidirectional_retention(q.transpose(0, 2, 1, 3), k.transpose(0, 2, 1, 3),
                                       v.transpose(0, 2, 1, 3),
                                       ret_decay_fwd[layer], ret_decay_bwd[layer])
        ret = ret.transpose(0, 2, 1, 3)
        ret = ret * lax.rsqrt(jnp.mean(ret * ret, axis=-1, keepdims=True) + EPS)
        ret = ret.reshape(b, s, RET_DIM) * ret_head_norm[layer].astype(jnp.float32)
        ret = (ret * jax.nn.silu(g.astype(jnp.float32))).astype(x.dtype)

        pf = p.astype(jnp.float32).reshape(b, s, POOL_GROUPS, POOL_GROUP_DIM)
        pooled = jnp.stack([_centred_mean_minus_self(pf[:, :, gi], w)
                            for gi, w in enumerate(POOL_WINDOWS)], axis=2)
        pool_out = jnp.einsum('bsgc,gcd->bsgd', pooled, pool_w[layer].astype(jnp.float32))
        pool_out = (pool_out.reshape(b, s, POOL_DIM) * pool_scale[layer].astype(jnp.float32)).astype(x.dtype)

        x = x + jnp.concatenate([ret, pool_out], axis=-1) @ w_out[layer]

        x = x + 0.5 * _swiglu(_rmsnorm(x, ffn2_norm[layer]), ffn2_w_gate[layer],
                              ffn2_w_up[layer], ffn2_w_down[layer])
    return _rmsnorm(x, final_norm)
```

```python
import functools

import jax
import jax.numpy as jnp
from jax import lax
from jax.experimental import pallas as pl
from jax.experimental.pallas import tpu as pltpu

EPS = 1e-6
ROPE_BASE = 10000.0
POOL_WINDOWS = (2, 4, 8, 16)
POOL_HALO = 16

_F32 = jnp.float32
_BF16 = jnp.bfloat16
_VMEM_LIMIT_BYTES = 60 * 1024 * 1024


def _params(*semantics):
    return pltpu.CompilerParams(dimension_semantics=semantics, vmem_limit_bytes=_VMEM_LIMIT_BYTES)


def _rmsnorm(x, gain):
    return x * lax.rsqrt(jnp.mean(x * x, axis=-1, keepdims=True) + EPS) * gain


def _silu(g):
    return g * jax.nn.sigmoid(g)


def _ffn_kernel(x_ref, gain_ref, wg_ref, wu_ref, wd_ref, fgain_ref, o_ref, h_ref,
                *, n_ff, final_norm, row_chunk, col_chunk):
    j = pl.program_id(1)
    tm, d = o_ref.shape

    @pl.when(j == 0)
    def _():
        for r in range(0, tm, row_chunk):
            rows = slice(r, r + row_chunk)
            h_ref[rows, :] = _rmsnorm(x_ref[rows, :], gain_ref[...]).astype(_BF16)
        o_ref[...] = jnp.zeros_like(o_ref)

    h = h_ref[...]
    g = jnp.dot(h, wg_ref[...], preferred_element_type=_F32)
    u = jnp.dot(h, wu_ref[...], preferred_element_type=_F32)
    a = (_silu(g) * u).astype(_BF16)
    for c in range(0, d, col_chunk):
        cols = slice(c, c + col_chunk)
        o_ref[:, cols] += jnp.dot(a, wd_ref[:, cols], preferred_element_type=_F32)

    @pl.when(j == n_ff - 1)
    def _():
        for r in range(0, tm, row_chunk):
            rows = slice(r, r + row_chunk)
            y = x_ref[rows, :] + 0.5 * o_ref[rows, :]
            if final_norm:
                y = _rmsnorm(y, fgain_ref[...])
            o_ref[rows, :] = y


def _ffn(x, gain, wg, wu, wd, fgain, *, final_norm, tm=512, tf=256):
    s, d = x.shape
    f = wg.shape[1]
    assert s % tm == 0 and f % tf == 0
    n_ff = f // tf
    return pl.pallas_call(
        functools.partial(_ffn_kernel, n_ff=n_ff, final_norm=final_norm, row_chunk=128, col_chunk=1024),
        grid=(s // tm, n_ff),
        in_specs=[
            pl.BlockSpec((tm, d), lambda i, j: (i, 0)),
            pl.BlockSpec((1, d), lambda i, j: (0, 0)),
            pl.BlockSpec((d, tf), lambda i, j: (0, j)),
            pl.BlockSpec((d, tf), lambda i, j: (0, j)),
            pl.BlockSpec((tf, d), lambda i, j: (j, 0)),
            pl.BlockSpec((1, d), lambda i, j: (0, 0)),
        ],
        out_specs=pl.BlockSpec((tm, d), lambda i, j: (i, 0)),
        out_shape=jax.ShapeDtypeStruct((s, d), _F32),
        scratch_shapes=[pltpu.VMEM((tm, d), _BF16)],
        compiler_params=_params("parallel", "arbitrary"),
        name="ffn",
    )(x, gain, wg, wu, wd, fgain)


def _proj_in_kernel(x_ref, gain_ref, pos_ref, freq_ref, w_ref, z_ref, h_ref, cos_ref, sin_ref,
                    *, n_q_blocks, n_rot_blocks, heads_per_block, head_dim, k_scale):
    j = pl.program_id(1)
    half = head_dim // 2

    @pl.when(j == 0)
    def _():
        h_ref[...] = _rmsnorm(x_ref[...], gain_ref[...]).astype(_BF16)
        ang = pos_ref[...].astype(_F32) * freq_ref[...]
        cos_ref[...] = jnp.cos(ang)
        sin_ref[...] = jnp.sin(ang)

    z = jnp.dot(h_ref[...], w_ref[...], preferred_element_type=_F32)

    @pl.when(j < n_rot_blocks)
    def _():
        cos = cos_ref[...]
        sin = sin_ref[...]
        scale = jnp.where(j >= n_q_blocks, k_scale, 1.0).astype(_F32)
        for t in range(heads_per_block):
            lo = t * head_dim
            t1 = z[:, lo:lo + half]
            t2 = z[:, lo + half:lo + head_dim]
            z_ref[:, lo:lo + half] = (t1 * cos - t2 * sin) * scale
            z_ref[:, lo + half:lo + head_dim] = (t1 * sin + t2 * cos) * scale

    @pl.when(j >= n_rot_blocks)
    def _():
        z_ref[...] = z


def _proj_in(x, gain, pos, freqs, w, *, ret_dim, head_dim, tm=512, tn=1024):
    s, d = x.shape
    n = w.shape[1]
    assert s % tm == 0 and n % tn == 0 and ret_dim % tn == 0 and tn % head_dim == 0
    half = head_dim // 2
    kern = functools.partial(
        _proj_in_kernel, n_q_blocks=ret_dim // tn, n_rot_blocks=2 * ret_dim // tn,
        heads_per_block=tn // head_dim, head_dim=head_dim, k_scale=head_dim ** -0.5)
    return pl.pallas_call(
        kern,
        grid=(s // tm, n // tn),
        in_specs=[
            pl.BlockSpec((tm, d), lambda i, j: (i, 0)),
            pl.BlockSpec((1, d), lambda i, j: (0, 0)),
            pl.BlockSpec((tm, 1), lambda i, j: (i, 0)),
            pl.BlockSpec((1, half), lambda i, j: (0, 0)),
            pl.BlockSpec((d, tn), lambda i, j: (0, j)),
        ],
        out_specs=pl.BlockSpec((tm, tn), lambda i, j: (i, j)),
        out_shape=jax.ShapeDtypeStruct((s, n), _F32),
        scratch_shapes=[pltpu.VMEM((tm, d), _BF16), pltpu.VMEM((tm, half), _F32),
                        pltpu.VMEM((tm, half), _F32)],
        compiler_params=_params("parallel", "arbitrary"),
        name="proj_in",
    )(x, gain, pos, freqs, w)


def _row_index(chunk):
    return lax.broadcasted_iota(jnp.int32, (chunk, 1), 0).astype(_F32)


def _ret_state_kernel(lgb_ref, k_ref, v_ref, b_ref, state_ref, *, chunk):
    t = pl.program_id(1)

    @pl.when(t == 0)
    def _():
        state_ref[...] = jnp.zeros_like(state_ref)

    b_ref[0, 0] = state_ref[...].astype(_BF16)
    lgb = jax.nn.log_sigmoid(lgb_ref[0])
    k_dec = jnp.exp(_row_index(chunk) * lgb)
    kk = (k_ref[...] * k_dec).astype(_BF16)
    upd = lax.dot_general(kk, v_ref[...].astype(_BF16), (((0,), (0,)), ((), ())),
                          preferred_element_type=_F32)
    state_ref[...] = state_ref[...] * jnp.exp(chunk * lgb) + upd


def _ret_out_kernel(lgf_ref, lgb_ref, q_ref, k_ref, v_ref, g_ref, b_ref, hn_ref, o_ref,
                    state_ref, dm_ref, *, chunk):
    c = pl.program_id(1)
    lgf = jax.nn.log_sigmoid(lgf_ref[0])
    lgb = jax.nn.log_sigmoid(lgb_ref[0])

    @pl.when(c == 0)
    def _():
        state_ref[...] = jnp.zeros_like(state_ref)
        rel = (lax.broadcasted_iota(jnp.int32, (chunk, chunk), 0)
               - lax.broadcasted_iota(jnp.int32, (chunk, chunk), 1)).astype(_F32)
        dm_ref[...] = jnp.exp(jnp.where(rel >= 0, rel * lgf, -rel * lgb))

    idx = _row_index(chunk)
    qb = q_ref[...].astype(_BF16)
    k = k_ref[...]
    kb = k.astype(_BF16)
    vb = v_ref[...].astype(_BF16)

    scores = lax.dot_general(qb, kb, (((1,), (1,)), ((), ())), preferred_element_type=_F32)
    scores = scores * dm_ref[...]
    out = jnp.dot(scores.astype(_BF16), vb, preferred_element_type=_F32)
    cross_f = jnp.dot(qb, state_ref[...].astype(_BF16), preferred_element_type=_F32)
    out += cross_f * jnp.exp((idx + 1.0) * lgf)
    cross_b = jnp.dot(qb, b_ref[0, 0], preferred_element_type=_F32)
    out += cross_b * jnp.exp((chunk - idx) * lgb)

    out = out * lax.rsqrt(jnp.mean(out * out, axis=-1, keepdims=True) + EPS)
    out = out * hn_ref[...]
    o_ref[...] = (out * _silu(g_ref[...])).astype(o_ref.dtype)

    kk = (k * jnp.exp((chunk - 1.0 - idx) * lgf)).astype(_BF16)
    upd = lax.dot_general(kk, vb, (((0,), (0,)), ((), ())), preferred_element_type=_F32)
    state_ref[...] = state_ref[...] * jnp.exp(chunk * lgf) + upd


def _retention(z, logit_f, logit_b, head_gain, *, heads, head_dim, chunk=512):
    s = z.shape[0]
    assert s % chunk == 0
    nc = s // chunk
    hd = head_dim
    decay_spec = pl.BlockSpec((1, 1, 1), lambda h, c: (h, 0, 0))

    b_states = pl.pallas_call(
        functools.partial(_ret_state_kernel, chunk=chunk),
        grid=(heads, nc),
        in_specs=[
            decay_spec,
            pl.BlockSpec((chunk, hd), lambda h, t: (nc - 1 - t, heads + h)),
            pl.BlockSpec((chunk, hd), lambda h, t: (nc - 1 - t, 2 * heads + h)),
        ],
        out_specs=pl.BlockSpec((1, 1, hd, hd), lambda h, t: (h, nc - 1 - t, 0, 0)),
        out_shape=jax.ShapeDtypeStruct((heads, nc, hd, hd), _BF16),
        scratch_shapes=[pltpu.VMEM((hd, hd), _F32)],
        compiler_params=_params("parallel", "arbitrary"),
        name="ret_state",
    )(logit_b, z, z)

    return pl.pallas_call(
        functools.partial(_ret_out_kernel, chunk=chunk),
        grid=(heads, nc),
        in_specs=[
            decay_spec,
            decay_spec,
            pl.BlockSpec((chunk, hd), lambda h, c: (c, h)),
            pl.BlockSpec((chunk, hd), lambda h, c: (c, heads + h)),
            pl.BlockSpec((chunk, hd), lambda h, c: (c, 2 * heads + h)),
            pl.BlockSpec((chunk, hd), lambda h, c: (c, 3 * heads + h)),
            pl.BlockSpec((1, 1, hd, hd), lambda h, c: (h, c, 0, 0)),
            pl.BlockSpec((1, hd), lambda h, c: (0, h)),
        ],
        out_specs=pl.BlockSpec((chunk, hd), lambda h, c: (c, h)),
        out_shape=jax.ShapeDtypeStruct((s, heads * hd), _BF16),
        scratch_shapes=[pltpu.VMEM((hd, hd), _F32), pltpu.VMEM((chunk, chunk), _F32)],
        compiler_params=_params("parallel", "arbitrary"),
        name="ret_out",
    )(logit_f, logit_b, z, z, z, z, b_states, head_gain)


def _pool_kernel(prev_ref, cur_ref, next_ref, w_ref, scale_ref, o_ref, pad_ref, *, tm, seq, group_dim):
    i = pl.program_id(0)
    n_blocks = pl.num_programs(0)
    halo = POOL_HALO
    pad_ref[0:halo, :] = jnp.where(i > 0, prev_ref[...], 0.0)
    pad_ref[halo:halo + tm, :] = cur_ref[...]
    pad_ref[halo + tm:, :] = jnp.where(i < n_blocks - 1, next_ref[...], 0.0)

    row = i * tm + lax.broadcasted_iota(jnp.int32, (tm, 1), 0)
    for gi, window in enumerate(POOL_WINDOWS):
        lo = window // 2
        hi = window - 1 - lo
        cols = slice(gi * group_dim, (gi + 1) * group_dim)
        total = pad_ref[halo - lo:halo - lo + tm, cols]
        for off in range(-lo + 1, hi + 1):
            total += pad_ref[halo + off:halo + off + tm, cols]
        count = (jnp.minimum(row + hi + 1, seq) - jnp.maximum(row - lo, 0)).astype(_F32)
        pooled = total / count - cur_ref[:, cols]
        mapped = jnp.dot(pooled.astype(_BF16), w_ref[gi], preferred_element_type=_F32)
        o_ref[:, cols] = (mapped * scale_ref[:, cols]).astype(o_ref.dtype)


def _pool(z, pool_w, pool_scale, *, tm=512):
    s, n = z.shape
    groups, group_dim, _ = pool_w.shape
    pool_dim = groups * group_dim
    assert groups == len(POOL_WINDOWS) and max(POOL_WINDOWS) // 2 <= POOL_HALO
    assert s % tm == 0 and tm % POOL_HALO == 0 and n % pool_dim == 0
    col = n // pool_dim - 1
    r = tm // POOL_HALO
    last = s // POOL_HALO - 1
    return pl.pallas_call(
        functools.partial(_pool_kernel, tm=tm, seq=s, group_dim=group_dim),
        grid=(s // tm,),
        in_specs=[
            pl.BlockSpec((POOL_HALO, pool_dim), lambda i: (jnp.maximum(i * r - 1, 0), col)),
            pl.BlockSpec((tm, pool_dim), lambda i: (i, col)),
            pl.BlockSpec((POOL_HALO, pool_dim), lambda i: (jnp.minimum((i + 1) * r, last), col)),
            pl.BlockSpec((groups, group_dim, group_dim), lambda i: (0, 0, 0)),
            pl.BlockSpec((1, pool_dim), lambda i: (0, 0)),
        ],
        out_specs=pl.BlockSpec((tm, pool_dim), lambda i: (i, 0)),
        out_shape=jax.ShapeDtypeStruct((s, pool_dim), _BF16),
        scratch_shapes=[pltpu.VMEM((tm + 2 * POOL_HALO, pool_dim), _F32)],
        compiler_params=_params("parallel"),
        name="pool",
    )(z, z, z, pool_w, pool_scale)


def _proj_out_kernel(x_ref, ret_ref, pool_ref, wr_ref, wp_ref, o_ref):
    acc = jnp.dot(ret_ref[...], wr_ref[...], preferred_element_type=_F32)
    acc += jnp.dot(pool_ref[...], wp_ref[...], preferred_element_type=_F32)
    o_ref[...] = x_ref[...] + acc


def _proj_out(x, ret, pool, w, *, tm=1024, tn=1024):
    s, d = x.shape
    kr = ret.shape[1]
    kp = pool.shape[1]
    assert kr == kp and w.shape == (kr + kp, d) and s % tm == 0 and d % tn == 0
    return pl.pallas_call(
        _proj_out_kernel,
        grid=(s // tm, d // tn),
        in_specs=[
            pl.BlockSpec((tm, tn), lambda i, j: (i, j)),
            pl.BlockSpec((tm, kr), lambda i, j: (i, 0)),
            pl.BlockSpec((tm, kp), lambda i, j: (i, 0)),
            pl.BlockSpec((kr, tn), lambda i, j: (0, j)),
            pl.BlockSpec((kp, tn), lambda i, j: (1, j)),
        ],
        out_specs=pl.BlockSpec((tm, tn), lambda i, j: (i, j)),
        out_shape=jax.ShapeDtypeStruct((s, d), _F32),
        compiler_params=_params("parallel", "arbitrary"),
        name="proj_out",
    )(x, ret, pool, w, w)


def kernel(x, positions, ffn1_norm, ffn1_w_gate, ffn1_w_up, ffn1_w_down, mix_norm, w_in,
           ret_decay_fwd, ret_decay_bwd, ret_head_norm, pool_w, pool_scale, w_out,
           ffn2_norm, ffn2_w_gate, ffn2_w_up, ffn2_w_down, final_norm):
    b, s, d = x.shape
    assert b == 1, "retention scan is written for a single sequence"
    depth = ffn1_norm.shape[0]
    heads = ret_decay_fwd.shape[1]
    ret_dim = ret_head_norm.shape[1]
    head_dim = ret_dim // heads

    freqs = 1.0 / (ROPE_BASE ** (jnp.arange(0, head_dim, 2, dtype=_F32) / head_dim))
    freqs = freqs.reshape(1, head_dim // 2)
    pos = positions.reshape(s, 1)
    final_gain = final_norm.reshape(1, d)
    row = lambda v: v.reshape(1, -1).astype(_F32)
    bf = lambda w: w.astype(_BF16)

    xs = x.reshape(s, d)
    for layer in range(depth):
        last = layer == depth - 1
        xs = _ffn(xs, row(ffn1_norm[layer]), bf(ffn1_w_gate[layer]), bf(ffn1_w_up[layer]),
                  bf(ffn1_w_down[layer]), final_gain, final_norm=False)
        z = _proj_in(xs, row(mix_norm[layer]), pos, freqs, bf(w_in[layer]),
                     ret_dim=ret_dim, head_dim=head_dim)
        ret = _retention(z, ret_decay_fwd[layer].reshape(heads, 1, 1).astype(_F32),
                         ret_decay_bwd[layer].reshape(heads, 1, 1).astype(_F32),
                         row(ret_head_norm[layer]), heads=heads, head_dim=head_dim)
        pool = _pool(z, bf(pool_w[layer]), row(pool_scale[layer]))
        xs = _proj_out(xs, ret, pool, bf(w_out[layer]))
        xs = _ffn(xs, row(ffn2_norm[layer]), bf(ffn2_w_gate[layer]), bf(ffn2_w_up[layer]),
                  bf(ffn2_w_down[layer]), final_gain, final_norm=last)
    return xs.reshape(b, s, d)
```

```python
import functools

import jax
import jax.numpy as jnp
from jax import lax
from jax.experimental import pallas as pl
from jax.experimental.pallas import tpu as pltpu

EPS = 1e-6
ROPE_BASE = 10000.0
POOL_WINDOWS = (2, 4, 8, 16)
POOL_HALO = 16

_F32 = jnp.float32
_BF16 = jnp.bfloat16
_VMEM_LIMIT_BYTES = 60 * 1024 * 1024


def _params(*semantics):
    return pltpu.CompilerParams(dimension_semantics=semantics, vmem_limit_bytes=_VMEM_LIMIT_BYTES)


def _rmsnorm(x, gain):
    return x * lax.rsqrt(jnp.mean(x * x, axis=-1, keepdims=True) + EPS) * gain


def _silu(g):
    return g * jax.nn.sigmoid(g)


def _ffn_kernel(x_ref, gain_ref, wgu_ref, wd_ref, fgain_ref, o_ref, h_ref, a_ref,
                *, n_ff, tf, final_norm, row_chunk, col_chunk):
    j = pl.program_id(1)
    tm, d = o_ref.shape

    def gate_up(slot):
        gu = jnp.dot(h_ref[...], wgu_ref[0], preferred_element_type=_F32)
        a_ref[slot] = (_silu(gu[:, :tf]) * gu[:, tf:]).astype(_BF16)

    def down(slot):
        a = a_ref[slot]
        for c in range(0, d, col_chunk):
            cols = slice(c, c + col_chunk)
            o_ref[:, cols] += jnp.dot(a, wd_ref[:, cols], preferred_element_type=_F32)

    @pl.when(j == 0)
    def _():
        for r in range(0, tm, row_chunk):
            rows = slice(r, r + row_chunk)
            h_ref[rows, :] = _rmsnorm(x_ref[rows, :], gain_ref[...]).astype(_BF16)
        o_ref[...] = jnp.zeros_like(o_ref)
        gate_up(0)

    for parity in (0, 1):
        @pl.when(jnp.logical_and(jnp.logical_and(j > 0, j < n_ff), j % 2 == parity))
        def _():
            gate_up(parity)
            down(1 - parity)

    @pl.when(j == n_ff)
    def _():
        down((n_ff + 1) % 2)
        for r in range(0, tm, row_chunk):
            rows = slice(r, r + row_chunk)
            y = x_ref[rows, :] + 0.5 * o_ref[rows, :]
            if final_norm:
                y = _rmsnorm(y, fgain_ref[...])
            o_ref[rows, :] = y


def _ffn_weights(wg, wu, wd, tf):
    d, f = wg.shape
    n_ff = f // tf
    blocked = lambda w: w.astype(_BF16).reshape(d, n_ff, tf).transpose(1, 0, 2)
    return jnp.concatenate([blocked(wg), blocked(wu)], axis=-1), wd.astype(_BF16)


def _ffn(x, gain, wg, wu, wd, fgain, *, final_norm, tm=512, tf=256):
    s, d = x.shape
    f = wg.shape[1]
    assert s % tm == 0 and f % tf == 0
    n_ff = f // tf
    wgu, wd = _ffn_weights(wg, wu, wd, tf)
    return pl.pallas_call(
        functools.partial(_ffn_kernel, n_ff=n_ff, tf=tf, final_norm=final_norm, row_chunk=128,
                          col_chunk=1024),
        grid=(s // tm, n_ff + 1),
        in_specs=[
            pl.BlockSpec((tm, d), lambda i, j: (i, 0)),
            pl.BlockSpec((1, d), lambda i, j: (0, 0)),
            pl.BlockSpec((1, d, 2 * tf), lambda i, j: (jnp.minimum(j, n_ff - 1), 0, 0)),
            pl.BlockSpec((tf, d), lambda i, j: (jnp.maximum(j - 1, 0), 0)),
            pl.BlockSpec((1, d), lambda i, j: (0, 0)),
        ],
        out_specs=pl.BlockSpec((tm, d), lambda i, j: (i, 0)),
        out_shape=jax.ShapeDtypeStruct((s, d), _F32),
        scratch_shapes=[pltpu.VMEM((tm, d), _BF16), pltpu.VMEM((2, tm, tf), _BF16)],
        compiler_params=_params("parallel", "arbitrary"),
        name="ffn",
    )(x, gain, wgu, wd, fgain)


def _proj_in_kernel(x_ref, gain_ref, pos_ref, freq_ref, w_ref, z_ref, h_ref, cos_ref, sin_ref,
                    *, n_q_blocks, n_rot_blocks, heads_per_block, head_dim, k_scale):
    j = pl.program_id(1)
    half = head_dim // 2

    @pl.when(j == 0)
    def _():
        h_ref[...] = _rmsnorm(x_ref[...], gain_ref[...]).astype(_BF16)
        ang = pos_ref[...].astype(_F32) * freq_ref[...]
        cos_ref[...] = jnp.cos(ang)
        sin_ref[...] = jnp.sin(ang)

    z = jnp.dot(h_ref[...], w_ref[...], preferred_element_type=_F32)

    @pl.when(j < n_rot_blocks)
    def _():
        cos = cos_ref[...]
        sin = sin_ref[...]
        scale = jnp.where(j >= n_q_blocks, k_scale, 1.0).astype(_F32)
        for t in range(heads_per_block):
            lo = t * head_dim
            t1 = z[:, lo:lo + half]
            t2 = z[:, lo + half:lo + head_dim]
            z_ref[:, lo:lo + half] = (t1 * cos - t2 * sin) * scale
            z_ref[:, lo + half:lo + head_dim] = (t1 * sin + t2 * cos) * scale

    @pl.when(j >= n_rot_blocks)
    def _():
        z_ref[...] = z


def _proj_in(x, gain, pos, freqs, w, *, ret_dim, head_dim, tm=512, tn=1024):
    s, d = x.shape
    n = w.shape[1]
    assert s % tm == 0 and n % tn == 0 and ret_dim % tn == 0 and tn % head_dim == 0
    half = head_dim // 2
    kern = functools.partial(
        _proj_in_kernel, n_q_blocks=ret_dim // tn, n_rot_blocks=2 * ret_dim // tn,
        heads_per_block=tn // head_dim, head_dim=head_dim, k_scale=head_dim ** -0.5)
    return pl.pallas_call(
        kern,
        grid=(s // tm, n // tn),
        in_specs=[
            pl.BlockSpec((tm, d), lambda i, j: (i, 0)),
            pl.BlockSpec((1, d), lambda i, j: (0, 0)),
            pl.BlockSpec((tm, 1), lambda i, j: (i, 0)),
            pl.BlockSpec((1, half), lambda i, j: (0, 0)),
            pl.BlockSpec((d, tn), lambda i, j: (0, j)),
        ],
        out_specs=pl.BlockSpec((tm, tn), lambda i, j: (i, j)),
        out_shape=jax.ShapeDtypeStruct((s, n), _F32),
        scratch_shapes=[pltpu.VMEM((tm, d), _BF16), pltpu.VMEM((tm, half), _F32),
                        pltpu.VMEM((tm, half), _F32)],
        compiler_params=_params("parallel", "arbitrary"),
        name="proj_in",
    )(x, gain, pos, freqs, w)


def _row_index(chunk):
    return lax.broadcasted_iota(jnp.int32, (chunk, 1), 0).astype(_F32)


def _ret_state_kernel(lgb_ref, k_ref, v_ref, b_ref, state_ref, *, chunk):
    t = pl.program_id(1)

    @pl.when(t == 0)
    def _():
        state_ref[...] = jnp.zeros_like(state_ref)

    b_ref[0, 0] = state_ref[...].astype(_BF16)
    lgb = jax.nn.log_sigmoid(lgb_ref[0])
    k_dec = jnp.exp(_row_index(chunk) * lgb)
    kk = (k_ref[...] * k_dec).astype(_BF16)
    upd = lax.dot_general(kk, v_ref[...].astype(_BF16), (((0,), (0,)), ((), ())),
                          preferred_element_type=_F32)
    state_ref[...] = state_ref[...] * jnp.exp(chunk * lgb) + upd


def _ret_out_kernel(lgf_ref, lgb_ref, q_ref, k_ref, v_ref, g_ref, b_ref, hn_ref, o_ref,
                    state_ref, dm_ref, *, chunk):
    c = pl.program_id(1)
    lgf = jax.nn.log_sigmoid(lgf_ref[0])
    lgb = jax.nn.log_sigmoid(lgb_ref[0])

    @pl.when(c == 0)
    def _():
        state_ref[...] = jnp.zeros_like(state_ref)
        rel = (lax.broadcasted_iota(jnp.int32, (chunk, chunk), 0)
               - lax.broadcasted_iota(jnp.int32, (chunk, chunk), 1)).astype(_F32)
        dm_ref[...] = jnp.exp(jnp.where(rel >= 0, rel * lgf, -rel * lgb))

    idx = _row_index(chunk)
    qb = q_ref[...].astype(_BF16)
    k = k_ref[...]
    kb = k.astype(_BF16)
    vb = v_ref[...].astype(_BF16)

    scores = lax.dot_general(qb, kb, (((1,), (1,)), ((), ())), preferred_element_type=_F32)
    scores = scores * dm_ref[...]
    out = jnp.dot(scores.astype(_BF16), vb, preferred_element_type=_F32)
    cross_f = jnp.dot(qb, state_ref[...].astype(_BF16), preferred_element_type=_F32)
    out += cross_f * jnp.exp((idx + 1.0) * lgf)
    cross_b = jnp.dot(qb, b_ref[0, 0], preferred_element_type=_F32)
    out += cross_b * jnp.exp((chunk - idx) * lgb)

    out = out * lax.rsqrt(jnp.mean(out * out, axis=-1, keepdims=True) + EPS)
    out = out * hn_ref[...]
    o_ref[...] = (out * _silu(g_ref[...])).astype(o_ref.dtype)

    kk = (k * jnp.exp((chunk - 1.0 - idx) * lgf)).astype(_BF16)
    upd = lax.dot_general(kk, vb, (((0,), (0,)), ((), ())), preferred_element_type=_F32)
    state_ref[...] = state_ref[...] * jnp.exp(chunk * lgf) + upd


def _retention(z, logit_f, logit_b, head_gain, *, heads, head_dim, chunk=512):
    s = z.shape[0]
    assert s % chunk == 0
    nc = s // chunk
    hd = head_dim
    decay_spec = pl.BlockSpec((1, 1, 1), lambda h, c: (h, 0, 0))

    b_states = pl.pallas_call(
        functools.partial(_ret_state_kernel, chunk=chunk),
        grid=(heads, nc),
        in_specs=[
            decay_spec,
            pl.BlockSpec((chunk, hd), lambda h, t: (nc - 1 - t, heads + h)),
            pl.BlockSpec((chunk, hd), lambda h, t: (nc - 1 - t, 2 * heads + h)),
        ],
        out_specs=pl.BlockSpec((1, 1, hd, hd), lambda h, t: (h, nc - 1 - t, 0, 0)),
        out_shape=jax.ShapeDtypeStruct((heads, nc, hd, hd), _BF16),
        scratch_shapes=[pltpu.VMEM((hd, hd), _F32)],
        compiler_params=_params("parallel", "arbitrary"),
        name="ret_state",
    )(logit_b, z, z)

    return pl.pallas_call(
        functools.partial(_ret_out_kernel, chunk=chunk),
        grid=(heads, nc),
        in_specs=[
            decay_spec,
            decay_spec,
            pl.BlockSpec((chunk, hd), lambda h, c: (c, h)),
            pl.BlockSpec((chunk, hd), lambda h, c: (c, heads + h)),
            pl.BlockSpec((chunk, hd), lambda h, c: (c, 2 * heads + h)),
            pl.BlockSpec((chunk, hd), lambda h, c: (c, 3 * heads + h)),
            pl.BlockSpec((1, 1, hd, hd), lambda h, c: (h, c, 0, 0)),
            pl.BlockSpec((1, hd), lambda h, c: (0, h)),
        ],
        out_specs=pl.BlockSpec((chunk, hd), lambda h, c: (c, h)),
        out_shape=jax.ShapeDtypeStruct((s, heads * hd), _BF16),
        scratch_shapes=[pltpu.VMEM((hd, hd), _F32), pltpu.VMEM((chunk, chunk), _F32)],
        compiler_params=_params("parallel", "arbitrary"),
        name="ret_out",
    )(logit_f, logit_b, z, z, z, z, b_states, head_gain)


def _pool_kernel(prev_ref, cur_ref, next_ref, w_ref, scale_ref, o_ref, pad_ref, *, tm, seq, group_dim):
    i = pl.program_id(0)
    n_blocks = pl.num_programs(0)
    halo = POOL_HALO
    pad_ref[0:halo, :] = jnp.where(i > 0, prev_ref[...], 0.0)
    pad_ref[halo:halo + tm, :] = cur_ref[...]
    pad_ref[halo + tm:, :] = jnp.where(i < n_blocks - 1, next_ref[...], 0.0)

    row = i * tm + lax.broadcasted_iota(jnp.int32, (tm, 1), 0)
    for gi, window in enumerate(POOL_WINDOWS):
        lo = window // 2
        hi = window - 1 - lo
        cols = slice(gi * group_dim, (gi + 1) * group_dim)
        total = pad_ref[halo - lo:halo - lo + tm, cols]
        for off in range(-lo + 1, hi + 1):
            total += pad_ref[halo + off:halo + off + tm, cols]
        count = (jnp.minimum(row + hi + 1, seq) - jnp.maximum(row - lo, 0)).astype(_F32)
        pooled = total / count - cur_ref[:, cols]
        mapped = jnp.dot(pooled.astype(_BF16), w_ref[gi], preferred_element_type=_F32)
        o_ref[:, cols] = (mapped * scale_ref[:, cols]).astype(o_ref.dtype)


def _pool(z, pool_w, pool_scale, *, tm=512):
    s, n = z.shape
    groups, group_dim, _ = pool_w.shape
    pool_dim = groups * group_dim
    assert groups == len(POOL_WINDOWS) and max(POOL_WINDOWS) // 2 <= POOL_HALO
    assert s % tm == 0 and tm % POOL_HALO == 0 and n % pool_dim == 0
    col = n // pool_dim - 1
    r = tm // POOL_HALO
    last = s // POOL_HALO - 1
    return pl.pallas_call(
        functools.partial(_pool_kernel, tm=tm, seq=s, group_dim=group_dim),
        grid=(s // tm,),
        in_specs=[
            pl.BlockSpec((POOL_HALO, pool_dim), lambda i: (jnp.maximum(i * r - 1, 0), col)),
            pl.BlockSpec((tm, pool_dim), lambda i: (i, col)),
            pl.BlockSpec((POOL_HALO, pool_dim), lambda i: (jnp.minimum((i + 1) * r, last), col)),
            pl.BlockSpec((groups, group_dim, group_dim), lambda i: (0, 0, 0)),
            pl.BlockSpec((1, pool_dim), lambda i: (0, 0)),
        ],
        out_specs=pl.BlockSpec((tm, pool_dim), lambda i: (i, 0)),
        out_shape=jax.ShapeDtypeStruct((s, pool_dim), _BF16),
        scratch_shapes=[pltpu.VMEM((tm + 2 * POOL_HALO, pool_dim), _F32)],
        compiler_params=_params("parallel"),
        name="pool",
    )(z, z, z, pool_w, pool_scale)


def _proj_out_kernel(x_ref, ret_ref, pool_ref, wr_ref, wp_ref, o_ref):
    acc = jnp.dot(ret_ref[...], wr_ref[...], preferred_element_type=_F32)
    acc += jnp.dot(pool_ref[...], wp_ref[...], preferred_element_type=_F32)
    o_ref[...] = x_ref[...] + acc


def _proj_out(x, ret, pool, w, *, tm=1024, tn=1024):
    s, d = x.shape
    kr = ret.shape[1]
    kp = pool.shape[1]
    assert kr == kp and w.shape == (kr + kp, d) and s % tm == 0 and d % tn == 0
    return pl.pallas_call(
        _proj_out_kernel,
        grid=(s // tm, d // tn),
        in_specs=[
            pl.BlockSpec((tm, tn), lambda i, j: (i, j)),
            pl.BlockSpec((tm, kr), lambda i, j: (i, 0)),
            pl.BlockSpec((tm, kp), lambda i, j: (i, 0)),
            pl.BlockSpec((kr, tn), lambda i, j: (0, j)),
            pl.BlockSpec((kp, tn), lambda i, j: (1, j)),
        ],
        out_specs=pl.BlockSpec((tm, tn), lambda i, j: (i, j)),
        out_shape=jax.ShapeDtypeStruct((s, d), _F32),
        compiler_params=_params("parallel", "arbitrary"),
        name="proj_out",
    )(x, ret, pool, w, w)


def kernel(x, positions, ffn1_norm, ffn1_w_gate, ffn1_w_up, ffn1_w_down, mix_norm, w_in,
           ret_decay_fwd, ret_decay_bwd, ret_head_norm, pool_w, pool_scale, w_out,
           ffn2_norm, ffn2_w_gate, ffn2_w_up, ffn2_w_down, final_norm):
    b, s, d = x.shape
    assert b == 1, "retention scan is written for a single sequence"
    depth = ffn1_norm.shape[0]
    heads = ret_decay_fwd.shape[1]
    ret_dim = ret_head_norm.shape[1]
    head_dim = ret_dim // heads

    freqs = 1.0 / (ROPE_BASE ** (jnp.arange(0, head_dim, 2, dtype=_F32) / head_dim))
    freqs = freqs.reshape(1, head_dim // 2)
    pos = positions.reshape(s, 1)
    final_gain = final_norm.reshape(1, d)
    row = lambda v: v.reshape(1, -1).astype(_F32)
    bf = lambda w: w.astype(_BF16)

    xs = x.reshape(s, d)
    for layer in range(depth):
        last = layer == depth - 1
        xs = _ffn(xs, row(ffn1_norm[layer]), ffn1_w_gate[layer], ffn1_w_up[layer],
                  ffn1_w_down[layer], final_gain, final_norm=False)
        z = _proj_in(xs, row(mix_norm[layer]), pos, freqs, bf(w_in[layer]),
                     ret_dim=ret_dim, head_dim=head_dim)
        ret = _retention(z, ret_decay_fwd[layer].reshape(heads, 1, 1).astype(_F32),
                         ret_decay_bwd[layer].reshape(heads, 1, 1).astype(_F32),
                         row(ret_head_norm[layer]), heads=heads, head_dim=head_dim)
        pool = _pool(z, bf(pool_w[layer]), row(pool_scale[layer]))
        xs = _proj_out(xs, ret, pool, bf(w_out[layer]))
        xs = _ffn(xs, row(ffn2_norm[layer]), ffn2_w_gate[layer], ffn2_w_up[layer],
                  ffn2_w_down[layer], final_gain, final_norm=last)
    return xs.reshape(b, s, d)
```

```python
import functools

import jax
import jax.numpy as jnp
from jax import lax
from jax.experimental import pallas as pl
from jax.experimental.pallas import tpu as pltpu

EPS = 1e-6
ROPE_BASE = 10000.0
POOL_WINDOWS = (2, 4, 8, 16)
POOL_HALO = 16

_F32 = jnp.float32
_BF16 = jnp.bfloat16
_VMEM_LIMIT_BYTES = 60 * 1024 * 1024


def _params(*semantics):
    return pltpu.CompilerParams(dimension_semantics=semantics, vmem_limit_bytes=_VMEM_LIMIT_BYTES)


def _rmsnorm(x, gain):
    return x * lax.rsqrt(jnp.mean(x * x, axis=-1, keepdims=True) + EPS) * gain


def _silu(g):
    return g * jax.nn.sigmoid(g)


def _ffn_kernel(x_hbm, gain_ref, wg_ref, wu_ref, wd_ref, fgain_ref, o_ref, x_buf, h_ref, x_sem,
                *, n_steps, n_row_blocks, tf, tail_subs, final_norm, row_chunk, col_chunk):
    i = pl.program_id(0)
    j = pl.program_id(1)
    tm, d = o_ref.shape

    def x_copy(block):
        return pltpu.make_async_copy(x_hbm.at[pl.ds(block * tm, tm), :], x_buf, x_sem)

    def activation(k):
        cols = slice(k * tf, (k + 1) * tf)
        h = h_ref[...]
        g = jnp.dot(h, wg_ref[:, cols], preferred_element_type=_F32)
        u = jnp.dot(h, wu_ref[:, cols], preferred_element_type=_F32)
        return (_silu(g) * u * 0.5).astype(_BF16)

    def step(n_subs):
        acts = [activation(k) for k in range(n_subs)]
        for k, a in enumerate(acts):
            for c in range(0, d, col_chunk):
                cols = slice(c, c + col_chunk)
                o_ref[:, cols] += jnp.dot(a, wd_ref[k * tf:(k + 1) * tf, cols],
                                          preferred_element_type=_F32)

    @pl.when(j == 0)
    def _():
        @pl.when(i == 0)
        def _():
            x_copy(0).start()

        x_copy(i).wait()
        for r in range(0, tm, row_chunk):
            rows = slice(r, r + row_chunk)
            xr = x_buf[rows, :]
            h_ref[rows, :] = _rmsnorm(xr, gain_ref[...]).astype(_BF16)
            o_ref[rows, :] = xr
        step(2)

    @pl.when(j == 1)
    def _():
        @pl.when(i + 1 < n_row_blocks)
        def _():
            x_copy(i + 1).start()

    @pl.when(jnp.logical_and(j > 0, j < n_steps - 1))
    def _():
        step(2)

    @pl.when(j == n_steps - 1)
    def _():
        step(tail_subs)
        if final_norm:
            for r in range(0, tm, row_chunk):
                rows = slice(r, r + row_chunk)
                o_ref[rows, :] = _rmsnorm(o_ref[rows, :], fgain_ref[...])


def _ffn(x, gain, wg, wu, wd, fgain, *, final_norm, tm=512, tf=256):
    s, d = x.shape
    f = wg.shape[1]
    assert s % tm == 0 and f % tf == 0
    n_steps = pl.cdiv(f, 2 * tf)
    assert n_steps >= 3
    tail_subs = (f - (n_steps - 1) * 2 * tf) // tf
    kern = functools.partial(
        _ffn_kernel, n_steps=n_steps, n_row_blocks=s // tm, tf=tf, tail_subs=tail_subs,
        final_norm=final_norm, row_chunk=128, col_chunk=1024)
    return pl.pallas_call(
        kern,
        grid=(s // tm, n_steps),
        in_specs=[
            pl.BlockSpec(memory_space=pl.ANY),
            pl.BlockSpec((1, d), lambda i, j: (0, 0)),
            pl.BlockSpec((d, 2 * tf), lambda i, j: (0, j)),
            pl.BlockSpec((d, 2 * tf), lambda i, j: (0, j)),
            pl.BlockSpec((2 * tf, d), lambda i, j: (j, 0)),
            pl.BlockSpec((1, d), lambda i, j: (0, 0)),
        ],
        out_specs=pl.BlockSpec((tm, d), lambda i, j: (i, 0)),
        out_shape=jax.ShapeDtypeStruct((s, d), _F32),
        scratch_shapes=[pltpu.VMEM((tm, d), _F32), pltpu.VMEM((tm, d), _BF16),
                        pltpu.SemaphoreType.DMA],
        compiler_params=_params("arbitrary", "arbitrary"),
        name="ffn",
    )(x, gain, wg.astype(_BF16), wu.astype(_BF16), wd.astype(_BF16), fgain)


def _proj_in_kernel(x_ref, gain_ref, pos_ref, freq_ref, w_ref, z_ref, h_ref, cos_ref, sin_ref,
                    *, n_q_blocks, n_rot_blocks, heads_per_block, head_dim, k_scale):
    j = pl.program_id(1)
    half = head_dim // 2

    @pl.when(j == 0)
    def _():
        h_ref[...] = _rmsnorm(x_ref[...], gain_ref[...]).astype(_BF16)
        ang = pos_ref[...].astype(_F32) * freq_ref[...]
        cos_ref[...] = jnp.cos(ang)
        sin_ref[...] = jnp.sin(ang)

    z = jnp.dot(h_ref[...], w_ref[...], preferred_element_type=_F32)

    @pl.when(j < n_rot_blocks)
    def _():
        cos = cos_ref[...]
        sin = sin_ref[...]
        scale = jnp.where(j >= n_q_blocks, k_scale, 1.0).astype(_F32)
        for t in range(heads_per_block):
            lo = t * head_dim
            t1 = z[:, lo:lo + half]
            t2 = z[:, lo + half:lo + head_dim]
            z_ref[:, lo:lo + half] = (t1 * cos - t2 * sin) * scale
            z_ref[:, lo + half:lo + head_dim] = (t1 * sin + t2 * cos) * scale

    @pl.when(j >= n_rot_blocks)
    def _():
        z_ref[...] = z


def _proj_in(x, gain, pos, freqs, w, *, ret_dim, head_dim, tm=512, tn=1024):
    s, d = x.shape
    n = w.shape[1]
    assert s % tm == 0 and n % tn == 0 and ret_dim % tn == 0 and tn % head_dim == 0
    half = head_dim // 2
    kern = functools.partial(
        _proj_in_kernel, n_q_blocks=ret_dim // tn, n_rot_blocks=2 * ret_dim // tn,
        heads_per_block=tn // head_dim, head_dim=head_dim, k_scale=head_dim ** -0.5)
    return pl.pallas_call(
        kern,
        grid=(s // tm, n // tn),
        in_specs=[
            pl.BlockSpec((tm, d), lambda i, j: (i, 0)),
            pl.BlockSpec((1, d), lambda i, j: (0, 0)),
            pl.BlockSpec((tm, 1), lambda i, j: (i, 0)),
            pl.BlockSpec((1, half), lambda i, j: (0, 0)),
            pl.BlockSpec((d, tn), lambda i, j: (0, j)),
        ],
        out_specs=pl.BlockSpec((tm, tn), lambda i, j: (i, j)),
        out_shape=jax.ShapeDtypeStruct((s, n), _F32),
        scratch_shapes=[pltpu.VMEM((tm, d), _BF16), pltpu.VMEM((tm, half), _F32),
                        pltpu.VMEM((tm, half), _F32)],
        compiler_params=_params("parallel", "arbitrary"),
        name="proj_in",
    )(x, gain, pos, freqs, w)


def _row_index(chunk):
    return lax.broadcasted_iota(jnp.int32, (chunk, 1), 0).astype(_F32)


def _ret_state_kernel(lgb_ref, k_ref, v_ref, b_ref, state_ref, *, chunk):
    t = pl.program_id(1)

    @pl.when(t == 0)
    def _():
        state_ref[...] = jnp.zeros_like(state_ref)

    b_ref[0, 0] = state_ref[...].astype(_BF16)
    lgb = jax.nn.log_sigmoid(lgb_ref[0])
    k_dec = jnp.exp(_row_index(chunk) * lgb)
    kk = (k_ref[...] * k_dec).astype(_BF16)
    upd = lax.dot_general(kk, v_ref[...].astype(_BF16), (((0,), (0,)), ((), ())),
                          preferred_element_type=_F32)
    state_ref[...] = state_ref[...] * jnp.exp(chunk * lgb) + upd


def _ret_out_kernel(lgf_ref, lgb_ref, q_ref, k_ref, v_ref, g_ref, b_ref, hn_ref, o_ref,
                    state_ref, dm_ref, *, chunk):
    c = pl.program_id(1)
    lgf = jax.nn.log_sigmoid(lgf_ref[0])
    lgb = jax.nn.log_sigmoid(lgb_ref[0])

    @pl.when(c == 0)
    def _():
        state_ref[...] = jnp.zeros_like(state_ref)
        rel = (lax.broadcasted_iota(jnp.int32, (chunk, chunk), 0)
               - lax.broadcasted_iota(jnp.int32, (chunk, chunk), 1)).astype(_F32)
        dm_ref[...] = jnp.exp(jnp.where(rel >= 0, rel * lgf, -rel * lgb))

    idx = _row_index(chunk)
    qb = q_ref[...].astype(_BF16)
    k = k_ref[...]
    kb = k.astype(_BF16)
    vb = v_ref[...].astype(_BF16)

    scores = lax.dot_general(qb, kb, (((1,), (1,)), ((), ())), preferred_element_type=_F32)
    scores = scores * dm_ref[...]
    out = jnp.dot(scores.astype(_BF16), vb, preferred_element_type=_F32)
    cross_f = jnp.dot(qb, state_ref[...].astype(_BF16), preferred_element_type=_F32)
    out += cross_f * jnp.exp((idx + 1.0) * lgf)
    cross_b = jnp.dot(qb, b_ref[0, 0], preferred_element_type=_F32)
    out += cross_b * jnp.exp((chunk - idx) * lgb)

    out = out * lax.rsqrt(jnp.mean(out * out, axis=-1, keepdims=True) + EPS)
    out = out * hn_ref[...]
    o_ref[...] = (out * _silu(g_ref[...])).astype(o_ref.dtype)

    kk = (k * jnp.exp((chunk - 1.0 - idx) * lgf)).astype(_BF16)
    upd = lax.dot_general(kk, vb, (((0,), (0,)), ((), ())), preferred_element_type=_F32)
    state_ref[...] = state_ref[...] * jnp.exp(chunk * lgf) + upd


def _retention(z, logit_f, logit_b, head_gain, *, heads, head_dim, chunk=512):
    s = z.shape[0]
    assert s % chunk == 0
    nc = s // chunk
    hd = head_dim
    decay_spec = pl.BlockSpec((1, 1, 1), lambda h, c: (h, 0, 0))

    b_states = pl.pallas_call(
        functools.partial(_ret_state_kernel, chunk=chunk),
        grid=(heads, nc),
        in_specs=[
            decay_spec,
            pl.BlockSpec((chunk, hd), lambda h, t: (nc - 1 - t, heads + h)),
            pl.BlockSpec((chunk, hd), lambda h, t: (nc - 1 - t, 2 * heads + h)),
        ],
        out_specs=pl.BlockSpec((1, 1, hd, hd), lambda h, t: (h, nc - 1 - t, 0, 0)),
        out_shape=jax.ShapeDtypeStruct((heads, nc, hd, hd), _BF16),
        scratch_shapes=[pltpu.VMEM((hd, hd), _F32)],
        compiler_params=_params("parallel", "arbitrary"),
        name="ret_state",
    )(logit_b, z, z)

    return pl.pallas_call(
        functools.partial(_ret_out_kernel, chunk=chunk),
        grid=(heads, nc),
        in_specs=[
            decay_spec,
            decay_spec,
            pl.BlockSpec((chunk, hd), lambda h, c: (c, h)),
            pl.BlockSpec((chunk, hd), lambda h, c: (c, heads + h)),
            pl.BlockSpec((chunk, hd), lambda h, c: (c, 2 * heads + h)),
            pl.BlockSpec((chunk, hd), lambda h, c: (c, 3 * heads + h)),
            pl.BlockSpec((1, 1, hd, hd), lambda h, c: (h, c, 0, 0)),
            pl.BlockSpec((1, hd), lambda h, c: (0, h)),
        ],
        out_specs=pl.BlockSpec((chunk, hd), lambda h, c: (c, h)),
        out_shape=jax.ShapeDtypeStruct((s, heads * hd), _BF16),
        scratch_shapes=[pltpu.VMEM((hd, hd), _F32), pltpu.VMEM((chunk, chunk), _F32)],
        compiler_params=_params("parallel", "arbitrary"),
        name="ret_out",
    )(logit_f, logit_b, z, z, z, z, b_states, head_gain)


def _pool_kernel(prev_ref, cur_ref, next_ref, w_ref, scale_ref, o_ref, pad_ref, *, tm, seq, group_dim):
    i = pl.program_id(0)
    n_blocks = pl.num_programs(0)
    halo = POOL_HALO
    pad_ref[0:halo, :] = jnp.where(i > 0, prev_ref[...], 0.0)
    pad_ref[halo:halo + tm, :] = cur_ref[...]
    pad_ref[halo + tm:, :] = jnp.where(i < n_blocks - 1, next_ref[...], 0.0)

    row = i * tm + lax.broadcasted_iota(jnp.int32, (tm, 1), 0)
    for gi, window in enumerate(POOL_WINDOWS):
        lo = window // 2
        hi = window - 1 - lo
        cols = slice(gi * group_dim, (gi + 1) * group_dim)
        total = pad_ref[halo - lo:halo - lo + tm, cols]
        for off in range(-lo + 1, hi + 1):
            total += pad_ref[halo + off:halo + off + tm, cols]
        count = (jnp.minimum(row + hi + 1, seq) - jnp.maximum(row - lo, 0)).astype(_F32)
        pooled = total / count - cur_ref[:, cols]
        mapped = jnp.dot(pooled.astype(_BF16), w_ref[gi], preferred_element_type=_F32)
        o_ref[:, cols] = (mapped * scale_ref[:, cols]).astype(o_ref.dtype)


def _pool(z, pool_w, pool_scale, *, tm=512):
    s, n = z.shape
    groups, group_dim, _ = pool_w.shape
    pool_dim = groups * group_dim
    assert groups == len(POOL_WINDOWS) and max(POOL_WINDOWS) // 2 <= POOL_HALO
    assert s % tm == 0 and tm % POOL_HALO == 0 and n % pool_dim == 0
    col = n // pool_dim - 1
    r = tm // POOL_HALO
    last = s // POOL_HALO - 1
    return pl.pallas_call(
        functools.partial(_pool_kernel, tm=tm, seq=s, group_dim=group_dim),
        grid=(s // tm,),
        in_specs=[
            pl.BlockSpec((POOL_HALO, pool_dim), lambda i: (jnp.maximum(i * r - 1, 0), col)),
            pl.BlockSpec((tm, pool_dim), lambda i: (i, col)),
            pl.BlockSpec((POOL_HALO, pool_dim), lambda i: (jnp.minimum((i + 1) * r, last), col)),
            pl.BlockSpec((groups, group_dim, group_dim), lambda i: (0, 0, 0)),
            pl.BlockSpec((1, pool_dim), lambda i: (0, 0)),
        ],
        out_specs=pl.BlockSpec((tm, pool_dim), lambda i: (i, 0)),
        out_shape=jax.ShapeDtypeStruct((s, pool_dim), _BF16),
        scratch_shapes=[pltpu.VMEM((tm + 2 * POOL_HALO, pool_dim), _F32)],
        compiler_params=_params("parallel"),
        name="pool",
    )(z, z, z, pool_w, pool_scale)


def _proj_out_kernel(x_ref, ret_ref, pool_ref, wr_ref, wp_ref, o_ref):
    acc = jnp.dot(ret_ref[...], wr_ref[...], preferred_element_type=_F32)
    acc += jnp.dot(pool_ref[...], wp_ref[...], preferred_element_type=_F32)
    o_ref[...] = x_ref[...] + acc


def _proj_out(x, ret, pool, w, *, tm=1024, tn=1024):
    s, d = x.shape
    kr = ret.shape[1]
    kp = pool.shape[1]
    assert kr == kp and w.shape == (kr + kp, d) and s % tm == 0 and d % tn == 0
    return pl.pallas_call(
        _proj_out_kernel,
        grid=(s // tm, d // tn),
        in_specs=[
            pl.BlockSpec((tm, tn), lambda i, j: (i, j)),
            pl.BlockSpec((tm, kr), lambda i, j: (i, 0)),
            pl.BlockSpec((tm, kp), lambda i, j: (i, 0)),
            pl.BlockSpec((kr, tn), lambda i, j: (0, j)),
            pl.BlockSpec((kp, tn), lambda i, j: (1, j)),
        ],
        out_specs=pl.BlockSpec((tm, tn), lambda i, j: (i, j)),
        out_shape=jax.ShapeDtypeStruct((s, d), _F32),
        compiler_params=_params("parallel", "arbitrary"),
        name="proj_out",
    )(x, ret, pool, w, w)


def kernel(x, positions, ffn1_norm, ffn1_w_gate, ffn1_w_up, ffn1_w_down, mix_norm, w_in,
           ret_decay_fwd, ret_decay_bwd, ret_head_norm, pool_w, pool_scale, w_out,
           ffn2_norm, ffn2_w_gate, ffn2_w_up, ffn2_w_down, final_norm):
    b, s, d = x.shape
    assert b == 1, "retention scan is written for a single sequence"
    depth = ffn1_norm.shape[0]
    heads = ret_decay_fwd.shape[1]
    ret_dim = ret_head_norm.shape[1]
    head_dim = ret_dim // heads

    freqs = 1.0 / (ROPE_BASE ** (jnp.arange(0, head_dim, 2, dtype=_F32) / head_dim))
    freqs = freqs.reshape(1, head_dim // 2)
    pos = positions.reshape(s, 1)
    final_gain = final_norm.reshape(1, d)
    row = lambda v: v.reshape(1, -1).astype(_F32)
    bf = lambda w: w.astype(_BF16)

    xs = x.reshape(s, d)
    for layer in range(depth):
        last = layer == depth - 1
        xs = _ffn(xs, row(ffn1_norm[layer]), ffn1_w_gate[layer], ffn1_w_up[layer],
                  ffn1_w_down[layer], final_gain, final_norm=False)
        z = _proj_in(xs, row(mix_norm[layer]), pos, freqs, bf(w_in[layer]),
                     ret_dim=ret_dim, head_dim=head_dim)
        ret = _retention(z, ret_decay_fwd[layer].reshape(heads, 1, 1).astype(_F32),
                         ret_decay_bwd[layer].reshape(heads, 1, 1).astype(_F32),
                         row(ret_head_norm[layer]), heads=heads, head_dim=head_dim)
        pool = _pool(z, bf(pool_w[layer]), row(pool_scale[layer]))
        xs = _proj_out(xs, ret, pool, bf(w_out[layer]))
        xs = _ffn(xs, row(ffn2_norm[layer]), ffn2_w_gate[layer], ffn2_w_up[layer],
                  ffn2_w_down[layer], final_gain, final_norm=last)
    return xs.reshape(b, s, d)
```

```python
import functools

import jax
import jax.numpy as jnp
from jax import lax
from jax.experimental import pallas as pl
from jax.experimental.pallas import tpu as pltpu

EPS = 1e-6
ROPE_BASE = 10000.0
POOL_WINDOWS = (2, 4, 8, 16)
POOL_HALO = 16

_F32 = jnp.float32
_BF16 = jnp.bfloat16
_VMEM_LIMIT_BYTES = 60 * 1024 * 1024


def _params(*semantics):
    return pltpu.CompilerParams(dimension_semantics=semantics, vmem_limit_bytes=_VMEM_LIMIT_BYTES)


def _rmsnorm(x, gain):
    return x * lax.rsqrt(jnp.mean(x * x, axis=-1, keepdims=True) + EPS) * gain


def _silu(g):
    return g * jax.nn.sigmoid(g)


def _ffn_kernel(x_hbm, gain_ref, wg_ref, wu_ref, wd_ref, fgain_ref, o_ref, x_buf, h_ref, x_sem,
                *, n_steps, n_row_blocks, tf, tail_subs, final_norm, row_chunk, col_chunk):
    i = pl.program_id(0)
    j = pl.program_id(1)
    tm, d = o_ref.shape

    def x_copy(block):
        return pltpu.make_async_copy(x_hbm.at[pl.ds(block * tm, tm), :], x_buf, x_sem)

    def activation(k):
        cols = slice(k * tf, (k + 1) * tf)
        h = h_ref[...]
        g = jnp.dot(h, wg_ref[:, cols], preferred_element_type=_F32)
        u = jnp.dot(h, wu_ref[:, cols], preferred_element_type=_F32)
        return (_silu(g) * u * 0.5).astype(_BF16)

    def step(n_subs):
        acts = [activation(k) for k in range(n_subs)]
        for k, a in enumerate(acts):
            for c in range(0, d, col_chunk):
                cols = slice(c, c + col_chunk)
                o_ref[:, cols] += jnp.dot(a, wd_ref[k * tf:(k + 1) * tf, cols],
                                          preferred_element_type=_F32)

    @pl.when(j == 0)
    def _():
        @pl.when(i == 0)
        def _():
            x_copy(0).start()

        x_copy(i).wait()
        for r in range(0, tm, row_chunk):
            rows = slice(r, r + row_chunk)
            xr = x_buf[rows, :]
            h_ref[rows, :] = _rmsnorm(xr, gain_ref[...]).astype(_BF16)
            o_ref[rows, :] = xr
        step(2)

    @pl.when(j == 1)
    def _():
        @pl.when(i + 1 < n_row_blocks)
        def _():
            x_copy(i + 1).start()

    @pl.when(jnp.logical_and(j > 0, j < n_steps - 1))
    def _():
        step(2)

    @pl.when(j == n_steps - 1)
    def _():
        step(tail_subs)
        if final_norm:
            for r in range(0, tm, row_chunk):
                rows = slice(r, r + row_chunk)
                o_ref[rows, :] = _rmsnorm(o_ref[rows, :], fgain_ref[...])


def _ffn(x, gain, wg, wu, wd, fgain, *, final_norm, tm=512, tf=256):
    s, d = x.shape
    f = wg.shape[1]
    assert s % tm == 0 and f % tf == 0
    n_steps = pl.cdiv(f, 2 * tf)
    assert n_steps >= 3
    tail_subs = (f - (n_steps - 1) * 2 * tf) // tf
    kern = functools.partial(
        _ffn_kernel, n_steps=n_steps, n_row_blocks=s // tm, tf=tf, tail_subs=tail_subs,
        final_norm=final_norm, row_chunk=128, col_chunk=1024)
    return pl.pallas_call(
        kern,
        grid=(s // tm, n_steps),
        in_specs=[
            pl.BlockSpec(memory_space=pl.ANY),
            pl.BlockSpec((1, d), lambda i, j: (0, 0)),
            pl.BlockSpec((d, 2 * tf), lambda i, j: (0, j)),
            pl.BlockSpec((d, 2 * tf), lambda i, j: (0, j)),
            pl.BlockSpec((2 * tf, d), lambda i, j: (j, 0)),
            pl.BlockSpec((1, d), lambda i, j: (0, 0)),
        ],
        out_specs=pl.BlockSpec((tm, d), lambda i, j: (i, 0)),
        out_shape=jax.ShapeDtypeStruct((s, d), _F32),
        scratch_shapes=[pltpu.VMEM((tm, d), _F32), pltpu.VMEM((tm, d), _BF16),
                        pltpu.SemaphoreType.DMA],
        compiler_params=_params("arbitrary", "arbitrary"),
        name="ffn",
    )(x, gain, wg.astype(_BF16), wu.astype(_BF16), wd.astype(_BF16), fgain)


def _proj_in_kernel(x_hbm, gain_ref, pos_ref, freq_ref, w_ref, zr_ref, zf_ref,
                    x_buf, h_ref, cos_ref, sin_ref, x_sem,
                    *, n_row_blocks, n_plain, n_q_blocks, heads_per_block, head_dim, k_scale, row_chunk):
    i = pl.program_id(0)
    j = pl.program_id(1)
    tm = x_buf.shape[0]
    half = head_dim // 2

    def x_copy(block):
        return pltpu.make_async_copy(x_hbm.at[pl.ds(block * tm, tm), :], x_buf, x_sem)

    def head_dot(t):
        cols = slice(t * head_dim, (t + 1) * head_dim)
        return jnp.dot(h_ref[...], w_ref[:, cols], preferred_element_type=_F32)

    def plain_step():
        for t in range(heads_per_block):
            zf_ref[:, t * head_dim:(t + 1) * head_dim] = head_dot(t)

    @pl.when(j == 0)
    def _():
        @pl.when(i == 0)
        def _():
            x_copy(0).start()

        x_copy(i).wait()
        for r in range(0, tm, row_chunk):
            rows = slice(r, r + row_chunk)
            h_ref[rows, :] = _rmsnorm(x_buf[rows, :], gain_ref[...]).astype(_BF16)
        ang = pos_ref[...].astype(_F32) * freq_ref[...]
        cos_ref[...] = jnp.cos(ang)
        sin_ref[...] = jnp.sin(ang)
        plain_step()

    @pl.when(j == 1)
    def _():
        @pl.when(i + 1 < n_row_blocks)
        def _():
            x_copy(i + 1).start()

    @pl.when(jnp.logical_and(j > 0, j < n_plain))
    def _():
        plain_step()

    @pl.when(j >= n_plain)
    def _():
        cos = cos_ref[...]
        sin = sin_ref[...]
        scale = jnp.where(j - n_plain >= n_q_blocks, k_scale, 1.0).astype(_F32)
        for t in range(heads_per_block):
            z = head_dot(t)
            t1 = z[:, :half]
            t2 = z[:, half:]
            lo = t * head_dim
            zr_ref[:, lo:lo + half] = ((t1 * cos - t2 * sin) * scale).astype(zr_ref.dtype)
            zr_ref[:, lo + half:lo + head_dim] = ((t1 * sin + t2 * cos) * scale).astype(zr_ref.dtype)


def _proj_in(x, gain, pos, freqs, w, *, ret_dim, head_dim, tm=1024, tn=1024):
    s, d = x.shape
    n = w.shape[1]
    assert s % tm == 0 and n % tn == 0 and ret_dim % tn == 0 and tn % head_dim == 0
    half = head_dim // 2
    n_rot = 2 * ret_dim // tn
    n_plain = n // tn - n_rot
    assert n_plain >= 2
    kern = functools.partial(
        _proj_in_kernel, n_row_blocks=s // tm, n_plain=n_plain, n_q_blocks=ret_dim // tn,
        heads_per_block=tn // head_dim, head_dim=head_dim, k_scale=head_dim ** -0.5, row_chunk=128)
    return pl.pallas_call(
        kern,
        grid=(s // tm, n // tn),
        in_specs=[
            pl.BlockSpec(memory_space=pl.ANY),
            pl.BlockSpec((1, d), lambda i, j: (0, 0)),
            pl.BlockSpec((tm, 1), lambda i, j: (i, 0)),
            pl.BlockSpec((1, half), lambda i, j: (0, 0)),
            pl.BlockSpec((d, tn), lambda i, j: (0, jnp.where(j < n_plain, j + n_rot, j - n_plain))),
        ],
        out_specs=[
            pl.BlockSpec((tm, tn), lambda i, j: (i, jnp.maximum(j - n_plain, 0))),
            pl.BlockSpec((tm, tn), lambda i, j: (i, jnp.minimum(j, n_plain - 1))),
        ],
        out_shape=[jax.ShapeDtypeStruct((s, n_rot * tn), _BF16),
                   jax.ShapeDtypeStruct((s, n_plain * tn), _F32)],
        scratch_shapes=[pltpu.VMEM((tm, d), _F32), pltpu.VMEM((tm, d), _BF16),
                        pltpu.VMEM((tm, half), _F32), pltpu.VMEM((tm, half), _F32),
                        pltpu.SemaphoreType.DMA],
        compiler_params=_params("arbitrary", "arbitrary"),
        name="proj_in",
    )(x, gain, pos, freqs, w)


def _contract_rows(a, b):
    return lax.dot_general(a, b, (((0,), (0,)), ((), ())), preferred_element_type=_F32)


def _ret_state_kernel(lgb_ref, k_ref, v_ref, b_ref, state_ref, vdec_ref, cdec_ref, *, chunk, heads, hd):
    t = pl.program_id(0)

    @pl.when(t == 0)
    def _():
        state_ref[...] = jnp.zeros_like(state_ref)
        idx = lax.broadcasted_iota(jnp.int32, (chunk, hd), 0).astype(_F32)
        for h in range(heads):
            lgb = jax.nn.log_sigmoid(lgb_ref[h])
            vdec_ref[h] = jnp.exp(idx * lgb)
            cdec_ref[h] = jnp.broadcast_to(jnp.exp(chunk * lgb), (1, hd))

    for h in range(heads):
        cols = slice(h * hd, (h + 1) * hd)
        b_ref[h, 0] = state_ref[h].astype(_BF16)
        v_dec = (v_ref[:, cols] * vdec_ref[h]).astype(_BF16)
        state_ref[h] = state_ref[h] * cdec_ref[h] + _contract_rows(k_ref[:, cols], v_dec)


def _ret_out_kernel(lgf_ref, lgb_ref, q_ref, k_ref, v_ref, g_ref, b_ref, hn_ref, o_ref,
                    state_ref, dm_ref, dec_ref, cdec_ref, *, chunk, heads, hd):
    c = pl.program_id(0)

    @pl.when(c == 0)
    def _():
        state_ref[...] = jnp.zeros_like(state_ref)
        rel = (lax.broadcasted_iota(jnp.int32, (chunk, chunk), 0)
               - lax.broadcasted_iota(jnp.int32, (chunk, chunk), 1)).astype(_F32)
        idx = lax.broadcasted_iota(jnp.int32, (chunk, hd), 0).astype(_F32)
        for h in range(heads):
            lgf = jax.nn.log_sigmoid(lgf_ref[h])
            lgb = jax.nn.log_sigmoid(lgb_ref[h])
            dm_ref[h] = jnp.exp(jnp.where(rel >= 0, rel * lgf, -rel * lgb))
            dec_ref[0, h] = jnp.exp((idx + 1.0) * lgf)
            dec_ref[1, h] = jnp.exp((chunk - idx) * lgb)
            dec_ref[2, h] = jnp.exp((chunk - 1.0 - idx) * lgf)
            cdec_ref[h] = jnp.broadcast_to(jnp.exp(chunk * lgf), (1, hd))

    for h in range(heads):
        cols = slice(h * hd, (h + 1) * hd)
        q = q_ref[:, cols]
        k = k_ref[:, cols]
        v = v_ref[:, cols]
        scores = lax.dot_general(q, k, (((1,), (1,)), ((), ())), preferred_element_type=_F32)
        scores = scores * dm_ref[h]
        out = jnp.dot(scores.astype(_BF16), v.astype(_BF16), preferred_element_type=_F32)
        out += jnp.dot(q, state_ref[h].astype(_BF16), preferred_element_type=_F32) * dec_ref[0, h]
        out += jnp.dot(q, b_ref[h, 0], preferred_element_type=_F32) * dec_ref[1, h]

        out = out * lax.rsqrt(jnp.mean(out * out, axis=-1, keepdims=True) + EPS)
        out = out * hn_ref[:, cols]
        o_ref[:, cols] = (out * _silu(g_ref[:, cols])).astype(o_ref.dtype)

        v_dec = (v * dec_ref[2, h]).astype(_BF16)
        state_ref[h] = state_ref[h] * cdec_ref[h] + _contract_rows(k, v_dec)


def _retention(zr, zf, logit_f, logit_b, head_gain, *, heads, head_dim, chunk=256):
    s = zr.shape[0]
    assert s % chunk == 0
    nc = s // chunk
    hd = head_dim
    rd = heads * hd
    decay_spec = pl.BlockSpec((heads, 1, 1), lambda c: (0, 0, 0))

    b_states = pl.pallas_call(
        functools.partial(_ret_state_kernel, chunk=chunk, heads=heads, hd=hd),
        grid=(nc,),
        in_specs=[
            decay_spec,
            pl.BlockSpec((chunk, rd), lambda t: (nc - 1 - t, 1)),
            pl.BlockSpec((chunk, rd), lambda t: (nc - 1 - t, 0)),
        ],
        out_specs=pl.BlockSpec((heads, 1, hd, hd), lambda t: (0, nc - 1 - t, 0, 0)),
        out_shape=jax.ShapeDtypeStruct((heads, nc, hd, hd), _BF16),
        scratch_shapes=[pltpu.VMEM((heads, hd, hd), _F32), pltpu.VMEM((heads, chunk, hd), _F32),
                        pltpu.VMEM((heads, 1, hd), _F32)],
        compiler_params=_params("arbitrary"),
        name="ret_state",
    )(logit_b, zr, zf)

    return pl.pallas_call(
        functools.partial(_ret_out_kernel, chunk=chunk, heads=heads, hd=hd),
        grid=(nc,),
        in_specs=[
            decay_spec,
            decay_spec,
            pl.BlockSpec((chunk, rd), lambda c: (c, 0)),
            pl.BlockSpec((chunk, rd), lambda c: (c, 1)),
            pl.BlockSpec((chunk, rd), lambda c: (c, 0)),
            pl.BlockSpec((chunk, rd), lambda c: (c, 1)),
            pl.BlockSpec((heads, 1, hd, hd), lambda c: (0, c, 0, 0)),
            pl.BlockSpec((1, rd), lambda c: (0, 0)),
        ],
        out_specs=pl.BlockSpec((chunk, rd), lambda c: (c, 0)),
        out_shape=jax.ShapeDtypeStruct((s, rd), _BF16),
        scratch_shapes=[pltpu.VMEM((heads, hd, hd), _F32), pltpu.VMEM((heads, chunk, chunk), _F32),
                        pltpu.VMEM((3, heads, chunk, hd), _F32), pltpu.VMEM((heads, 1, hd), _F32)],
        compiler_params=_params("arbitrary"),
        name="ret_out",
    )(logit_f, logit_b, zr, zr, zf, zf, b_states, head_gain)


def _pool_kernel(prev_ref, cur_ref, next_ref, w_ref, scale_ref, o_ref, pad_ref, *, tm, seq, group_dim):
    i = pl.program_id(0)
    n_blocks = pl.num_programs(0)
    halo = POOL_HALO
    pad_ref[0:halo, :] = jnp.where(i > 0, prev_ref[...], 0.0)
    pad_ref[halo:halo + tm, :] = cur_ref[...]
    pad_ref[halo + tm:, :] = jnp.where(i < n_blocks - 1, next_ref[...], 0.0)

    row = i * tm + lax.broadcasted_iota(jnp.int32, (tm, 1), 0)
    for gi, window in enumerate(POOL_WINDOWS):
        lo = window // 2
        hi = window - 1 - lo
        cols = slice(gi * group_dim, (gi + 1) * group_dim)
        total = pad_ref[halo - lo:halo - lo + tm, cols]
        for off in range(-lo + 1, hi + 1):
            total += pad_ref[halo + off:halo + off + tm, cols]
        count = (jnp.minimum(row + hi + 1, seq) - jnp.maximum(row - lo, 0)).astype(_F32)
        pooled = total / count - cur_ref[:, cols]
        mapped = jnp.dot(pooled.astype(_BF16), w_ref[gi], preferred_element_type=_F32)
        o_ref[:, cols] = (mapped * scale_ref[:, cols]).astype(o_ref.dtype)


def _pool(z, pool_w, pool_scale, *, tm=512):
    s, n = z.shape
    groups, group_dim, _ = pool_w.shape
    pool_dim = groups * group_dim
    assert groups == len(POOL_WINDOWS) and max(POOL_WINDOWS) // 2 <= POOL_HALO
    assert s % tm == 0 and tm % POOL_HALO == 0 and n % pool_dim == 0
    col = n // pool_dim - 1
    r = tm // POOL_HALO
    last = s // POOL_HALO - 1
    return pl.pallas_call(
        functools.partial(_pool_kernel, tm=tm, seq=s, group_dim=group_dim),
        grid=(s // tm,),
        in_specs=[
            pl.BlockSpec((POOL_HALO, pool_dim), lambda i: (jnp.maximum(i * r - 1, 0), col)),
            pl.BlockSpec((tm, pool_dim), lambda i: (i, col)),
            pl.BlockSpec((POOL_HALO, pool_dim), lambda i: (jnp.minimum((i + 1) * r, last), col)),
            pl.BlockSpec((groups, group_dim, group_dim), lambda i: (0, 0, 0)),
            pl.BlockSpec((1, pool_dim), lambda i: (0, 0)),
        ],
        out_specs=pl.BlockSpec((tm, pool_dim), lambda i: (i, 0)),
        out_shape=jax.ShapeDtypeStruct((s, pool_dim), _BF16),
        scratch_shapes=[pltpu.VMEM((tm + 2 * POOL_HALO, pool_dim), _F32)],
        compiler_params=_params("parallel"),
        name="pool",
    )(z, z, z, pool_w, pool_scale)


def _proj_out_kernel(x_ref, ret_ref, pool_ref, wr_ref, wp_ref, o_ref):
    acc = jnp.dot(ret_ref[...], wr_ref[...], preferred_element_type=_F32)
    acc += jnp.dot(pool_ref[...], wp_ref[...], preferred_element_type=_F32)
    o_ref[...] = x_ref[...] + acc


def _proj_out(x, ret, pool, w, *, tm=1024, tn=1024):
    s, d = x.shape
    kr = ret.shape[1]
    kp = pool.shape[1]
    assert kr == kp and w.shape == (kr + kp, d) and s % tm == 0 and d % tn == 0
    return pl.pallas_call(
        _proj_out_kernel,
        grid=(s // tm, d // tn),
        in_specs=[
            pl.BlockSpec((tm, tn), lambda i, j: (i, j)),
            pl.BlockSpec((tm, kr), lambda i, j: (i, 0)),
            pl.BlockSpec((tm, kp), lambda i, j: (i, 0)),
            pl.BlockSpec((kr, tn), lambda i, j: (0, j)),
            pl.BlockSpec((kp, tn), lambda i, j: (1, j)),
        ],
        out_specs=pl.BlockSpec((tm, tn), lambda i, j: (i, j)),
        out_shape=jax.ShapeDtypeStruct((s, d), _F32),
        compiler_params=_params("parallel", "arbitrary"),
        name="proj_out",
    )(x, ret, pool, w, w)


def kernel(x, positions, ffn1_norm, ffn1_w_gate, ffn1_w_up, ffn1_w_down, mix_norm, w_in,
           ret_decay_fwd, ret_decay_bwd, ret_head_norm, pool_w, pool_scale, w_out,
           ffn2_norm, ffn2_w_gate, ffn2_w_up, ffn2_w_down, final_norm):
    b, s, d = x.shape
    assert b == 1, "retention scan is written for a single sequence"
    depth = ffn1_norm.shape[0]
    heads = ret_decay_fwd.shape[1]
    ret_dim = ret_head_norm.shape[1]
    head_dim = ret_dim // heads

    freqs = 1.0 / (ROPE_BASE ** (jnp.arange(0, head_dim, 2, dtype=_F32) / head_dim))
    freqs = freqs.reshape(1, head_dim // 2)
    pos = positions.reshape(s, 1)
    final_gain = final_norm.reshape(1, d)
    row = lambda v: v.reshape(1, -1).astype(_F32)
    bf = lambda w: w.astype(_BF16)

    xs = x.reshape(s, d)
    for layer in range(depth):
        last = layer == depth - 1
        xs = _ffn(xs, row(ffn1_norm[layer]), ffn1_w_gate[layer], ffn1_w_up[layer],
                  ffn1_w_down[layer], final_gain, final_norm=False)
        zr, zf = _proj_in(xs, row(mix_norm[layer]), pos, freqs, bf(w_in[layer]),
                          ret_dim=ret_dim, head_dim=head_dim)
        ret = _retention(zr, zf, ret_decay_fwd[layer].reshape(heads, 1, 1).astype(_F32),
                         ret_decay_bwd[layer].reshape(heads, 1, 1).astype(_F32),
                         row(ret_head_norm[layer]), heads=heads, head_dim=head_dim)
        pool = _pool(zf, bf(pool_w[layer]), row(pool_scale[layer]))
        xs = _proj_out(xs, ret, pool, bf(w_out[layer]))
        xs = _ffn(xs, row(ffn2_norm[layer]), ffn2_w_gate[layer], ffn2_w_up[layer],
                  ffn2_w_down[layer], final_gain, final_norm=last)
    return xs.reshape(b, s, d)
```

```python
import functools

import jax
import jax.numpy as jnp
from jax import lax
from jax.experimental import pallas as pl
from jax.experimental.pallas import tpu as pltpu

EPS = 1e-6
ROPE_BASE = 10000.0
POOL_WINDOWS = (2, 4, 8, 16)
POOL_HALO = 16

_F32 = jnp.float32
_BF16 = jnp.bfloat16
_VMEM_LIMIT_BYTES = 60 * 1024 * 1024


def _params(*semantics):
    return pltpu.CompilerParams(dimension_semantics=semantics, vmem_limit_bytes=_VMEM_LIMIT_BYTES)


def _rmsnorm(x, gain):
    return x * lax.rsqrt(jnp.mean(x * x, axis=-1, keepdims=True) + EPS) * gain


def _silu(g):
    return g * jax.nn.sigmoid(g)


def _ffn_kernel(x_hbm, gain_ref, wg_ref, wu_ref, wd_ref, fgain_ref, o_hbm, acc_ref, h_ref, x_sem, o_sem,
                *, n_steps, n_row_blocks, final_norm, row_chunk, row_part, col_chunk):
    i = pl.program_id(0)
    j = pl.program_id(1)
    _, tm, d = acc_ref.shape
    slot = i % 2
    other = 1 - slot

    def x_copy(block, s):
        return pltpu.make_async_copy(x_hbm.at[pl.ds(block * tm, tm), :], acc_ref.at[s], x_sem.at[s])

    def o_copy(block, s):
        return pltpu.make_async_copy(acc_ref.at[s], o_hbm.at[pl.ds(block * tm, tm), :], o_sem.at[s])

    def activation(rows):
        h = h_ref[rows, :]
        g = jnp.dot(h, wg_ref[...], preferred_element_type=_F32)
        u = jnp.dot(h, wu_ref[...], preferred_element_type=_F32)
        return (_silu(g) * u * 0.5).astype(_BF16)

    def step():
        parts = [slice(r, r + row_part) for r in range(0, tm, row_part)]
        acts = [activation(rows) for rows in parts]
        for rows, a in zip(parts, acts):
            for c in range(0, d, col_chunk):
                cols = slice(c, c + col_chunk)
                acc_ref[slot, rows, cols] += jnp.dot(a, wd_ref[:, cols], preferred_element_type=_F32)

    @pl.when(j == 0)
    def _():
        @pl.when(i == 0)
        def _():
            x_copy(0, 0).start()

        x_copy(i, slot).wait()
        for r in range(0, tm, row_chunk):
            rows = slice(r, r + row_chunk)
            h_ref[rows, :] = _rmsnorm(acc_ref[slot, rows, :], gain_ref[...]).astype(_BF16)
        step()

    @pl.when(j == 2)
    def _():
        @pl.when(i > 0)
        def _():
            o_copy(i - 1, other).wait()

        @pl.when(i + 1 < n_row_blocks)
        def _():
            x_copy(i + 1, other).start()

    @pl.when(jnp.logical_and(j > 0, j < n_steps - 1))
    def _():
        step()

    @pl.when(j == n_steps - 1)
    def _():
        step()
        if final_norm:
            for r in range(0, tm, row_chunk):
                rows = slice(r, r + row_chunk)
                acc_ref[slot, rows, :] = _rmsnorm(acc_ref[slot, rows, :], fgain_ref[...])
        o_copy(i, slot).start()

        @pl.when(i == n_row_blocks - 1)
        def _():
            o_copy(i, slot).wait()


def _ffn(x, gain, wg, wu, wd, fgain, *, final_norm, tm=1024, tf=256):
    s, d = x.shape
    f = wg.shape[1]
    assert s % tm == 0 and f % tf == 0
    n_steps = f // tf
    assert n_steps >= 4
    kern = functools.partial(
        _ffn_kernel, n_steps=n_steps, n_row_blocks=s // tm, final_norm=final_norm,
        row_chunk=128, row_part=256, col_chunk=1024)
    return pl.pallas_call(
        kern,
        grid=(s // tm, n_steps),
        in_specs=[
            pl.BlockSpec(memory_space=pl.ANY),
            pl.BlockSpec((1, d), lambda i, j: (0, 0)),
            pl.BlockSpec((d, tf), lambda i, j: (0, j)),
            pl.BlockSpec((d, tf), lambda i, j: (0, j)),
            pl.BlockSpec((tf, d), lambda i, j: (j, 0)),
            pl.BlockSpec((1, d), lambda i, j: (0, 0)),
        ],
        out_specs=pl.BlockSpec(memory_space=pl.ANY),
        out_shape=jax.ShapeDtypeStruct((s, d), _F32),
        scratch_shapes=[pltpu.VMEM((2, tm, d), _F32), pltpu.VMEM((tm, d), _BF16),
                        pltpu.SemaphoreType.DMA((2,)), pltpu.SemaphoreType.DMA((2,))],
        compiler_params=_params("arbitrary", "arbitrary"),
        name="ffn",
    )(x, gain, wg.astype(_BF16), wu.astype(_BF16), wd.astype(_BF16), fgain)


def _proj_in_kernel(x_hbm, gain_ref, pos_ref, freq_ref, w_ref, zr_ref, zf_ref,
                    x_buf, h_ref, cos_ref, sin_ref, x_sem,
                    *, n_row_blocks, n_plain, n_q_blocks, heads_per_block, head_dim, k_scale, row_chunk):
    i = pl.program_id(0)
    j = pl.program_id(1)
    tm = x_buf.shape[0]
    half = head_dim // 2

    def x_copy(block):
        return pltpu.make_async_copy(x_hbm.at[pl.ds(block * tm, tm), :], x_buf, x_sem)

    def head_dot(t):
        cols = slice(t * head_dim, (t + 1) * head_dim)
        return jnp.dot(h_ref[...], w_ref[:, cols], preferred_element_type=_F32)

    def plain_step():
        for t in range(heads_per_block):
            zf_ref[:, t * head_dim:(t + 1) * head_dim] = head_dot(t)

    @pl.when(j == 0)
    def _():
        @pl.when(i == 0)
        def _():
            x_copy(0).start()

        x_copy(i).wait()
        for r in range(0, tm, row_chunk):
            rows = slice(r, r + row_chunk)
            h_ref[rows, :] = _rmsnorm(x_buf[rows, :], gain_ref[...]).astype(_BF16)
        ang = pos_ref[...].astype(_F32) * freq_ref[...]
        cos_ref[...] = jnp.cos(ang)
        sin_ref[...] = jnp.sin(ang)
        plain_step()

    @pl.when(j == 1)
    def _():
        @pl.when(i + 1 < n_row_blocks)
        def _():
            x_copy(i + 1).start()

    @pl.when(jnp.logical_and(j > 0, j < n_plain))
    def _():
        plain_step()

    @pl.when(j >= n_plain)
    def _():
        cos = cos_ref[...]
        sin = sin_ref[...]
        scale = jnp.where(j - n_plain >= n_q_blocks, k_scale, 1.0).astype(_F32)
        for t in range(heads_per_block):
            z = head_dot(t)
            t1 = z[:, :half]
            t2 = z[:, half:]
            lo = t * head_dim
            zr_ref[:, lo:lo + half] = ((t1 * cos - t2 * sin) * scale).astype(zr_ref.dtype)
            zr_ref[:, lo + half:lo + head_dim] = ((t1 * sin + t2 * cos) * scale).astype(zr_ref.dtype)


def _proj_in(x, gain, pos, freqs, w, *, ret_dim, head_dim, tm=1024, tn=1024):
    s, d = x.shape
    n = w.shape[1]
    assert s % tm == 0 and n % tn == 0 and ret_dim % tn == 0 and tn % head_dim == 0
    half = head_dim // 2
    n_rot = 2 * ret_dim // tn
    n_plain = n // tn - n_rot
    assert n_plain >= 2
    kern = functools.partial(
        _proj_in_kernel, n_row_blocks=s // tm, n_plain=n_plain, n_q_blocks=ret_dim // tn,
        heads_per_block=tn // head_dim, head_dim=head_dim, k_scale=head_dim ** -0.5, row_chunk=128)
    return pl.pallas_call(
        kern,
        grid=(s // tm, n // tn),
        in_specs=[
            pl.BlockSpec(memory_space=pl.ANY),
            pl.BlockSpec((1, d), lambda i, j: (0, 0)),
            pl.BlockSpec((tm, 1), lambda i, j: (i, 0)),
            pl.BlockSpec((1, half), lambda i, j: (0, 0)),
            pl.BlockSpec((d, tn), lambda i, j: (0, jnp.where(j < n_plain, j + n_rot, j - n_plain))),
        ],
        out_specs=[
            pl.BlockSpec((tm, tn), lambda i, j: (i, jnp.maximum(j - n_plain, 0))),
            pl.BlockSpec((tm, tn), lambda i, j: (i, jnp.minimum(j, n_plain - 1))),
        ],
        out_shape=[jax.ShapeDtypeStruct((s, n_rot * tn), _BF16),
                   jax.ShapeDtypeStruct((s, n_plain * tn), _F32)],
        scratch_shapes=[pltpu.VMEM((tm, d), _F32), pltpu.VMEM((tm, d), _BF16),
                        pltpu.VMEM((tm, half), _F32), pltpu.VMEM((tm, half), _F32),
                        pltpu.SemaphoreType.DMA],
        compiler_params=_params("arbitrary", "arbitrary"),
        name="proj_in",
    )(x, gain, pos, freqs, w)


def _contract_rows(a, b):
    return lax.dot_general(a, b, (((0,), (0,)), ((), ())), preferred_element_type=_F32)


def _ret_state_kernel(lgb_ref, k_ref, v_ref, b_ref, state_ref, vdec_ref, cdec_ref, *, chunk, heads, hd):
    t = pl.program_id(0)

    @pl.when(t == 0)
    def _():
        state_ref[...] = jnp.zeros_like(state_ref)
        idx = lax.broadcasted_iota(jnp.int32, (chunk, hd), 0).astype(_F32)
        for h in range(heads):
            lgb = jax.nn.log_sigmoid(lgb_ref[h])
            vdec_ref[h] = jnp.exp(idx * lgb)
            cdec_ref[h] = jnp.broadcast_to(jnp.exp(chunk * lgb), (1, hd))

    for h in range(heads):
        cols = slice(h * hd, (h + 1) * hd)
        b_ref[h, 0] = state_ref[h].astype(_BF16)
        v_dec = (v_ref[:, cols] * vdec_ref[h]).astype(_BF16)
        state_ref[h] = state_ref[h] * cdec_ref[h] + _contract_rows(k_ref[:, cols], v_dec)


def _ret_out_kernel(lgf_ref, lgb_ref, q_ref, k_ref, v_ref, g_ref, b_ref, hn_ref, o_ref,
                    state_ref, dm_ref, dec_ref, cdec_ref, *, chunk, heads, hd):
    c = pl.program_id(0)

    @pl.when(c == 0)
    def _():
        state_ref[...] = jnp.zeros_like(state_ref)
        rel = (lax.broadcasted_iota(jnp.int32, (chunk, chunk), 0)
               - lax.broadcasted_iota(jnp.int32, (chunk, chunk), 1)).astype(_F32)
        idx = lax.broadcasted_iota(jnp.int32, (chunk, hd), 0).astype(_F32)
        for h in range(heads):
            lgf = jax.nn.log_sigmoid(lgf_ref[h])
            lgb = jax.nn.log_sigmoid(lgb_ref[h])
            dm_ref[h] = jnp.exp(jnp.where(rel >= 0, rel * lgf, -rel * lgb))
            dec_ref[0, h] = jnp.exp((idx + 1.0) * lgf)
            dec_ref[1, h] = jnp.exp((chunk - idx) * lgb)
            dec_ref[2, h] = jnp.exp((chunk - 1.0 - idx) * lgf)
            cdec_ref[h] = jnp.broadcast_to(jnp.exp(chunk * lgf), (1, hd))

    for h in range(heads):
        cols = slice(h * hd, (h + 1) * hd)
        q = q_ref[:, cols]
        k = k_ref[:, cols]
        v = v_ref[:, cols]
        scores = lax.dot_general(q, k, (((1,), (1,)), ((), ())), preferred_element_type=_F32)
        scores = scores * dm_ref[h]
        out = jnp.dot(scores.astype(_BF16), v.astype(_BF16), preferred_element_type=_F32)
        out += jnp.dot(q, state_ref[h].astype(_BF16), preferred_element_type=_F32) * dec_ref[0, h]
        out += jnp.dot(q, b_ref[h, 0], preferred_element_type=_F32) * dec_ref[1, h]

        out = out * lax.rsqrt(jnp.mean(out * out, axis=-1, keepdims=True) + EPS)
        out = out * hn_ref[:, cols]
        o_ref[:, cols] = (out * _silu(g_ref[:, cols])).astype(o_ref.dtype)

        v_dec = (v * dec_ref[2, h]).astype(_BF16)
        state_ref[h] = state_ref[h] * cdec_ref[h] + _contract_rows(k, v_dec)


def _retention(zr, zf, logit_f, logit_b, head_gain, *, heads, head_dim, chunk=256):
    s = zr.shape[0]
    assert s % chunk == 0
    nc = s // chunk
    hd = head_dim
    rd = heads * hd
    decay_spec = pl.BlockSpec((heads, 1, 1), lambda c: (0, 0, 0))

    b_states = pl.pallas_call(
        functools.partial(_ret_state_kernel, chunk=chunk, heads=heads, hd=hd),
        grid=(nc,),
        in_specs=[
            decay_spec,
            pl.BlockSpec((chunk, rd), lambda t: (nc - 1 - t, 1)),
            pl.BlockSpec((chunk, rd), lambda t: (nc - 1 - t, 0)),
        ],
        out_specs=pl.BlockSpec((heads, 1, hd, hd), lambda t: (0, nc - 1 - t, 0, 0)),
        out_shape=jax.ShapeDtypeStruct((heads, nc, hd, hd), _BF16),
        scratch_shapes=[pltpu.VMEM((heads, hd, hd), _F32), pltpu.VMEM((heads, chunk, hd), _F32),
                        pltpu.VMEM((heads, 1, hd), _F32)],
        compiler_params=_params("arbitrary"),
        name="ret_state",
    )(logit_b, zr, zf)

    return pl.pallas_call(
        functools.partial(_ret_out_kernel, chunk=chunk, heads=heads, hd=hd),
        grid=(nc,),
        in_specs=[
            decay_spec,
            decay_spec,
            pl.BlockSpec((chunk, rd), lambda c: (c, 0)),
            pl.BlockSpec((chunk, rd), lambda c: (c, 1)),
            pl.BlockSpec((chunk, rd), lambda c: (c, 0)),
            pl.BlockSpec((chunk, rd), lambda c: (c, 1)),
            pl.BlockSpec((heads, 1, hd, hd), lambda c: (0, c, 0, 0)),
            pl.BlockSpec((1, rd), lambda c: (0, 0)),
        ],
        out_specs=pl.BlockSpec((chunk, rd), lambda c: (c, 0)),
        out_shape=jax.ShapeDtypeStruct((s, rd), _BF16),
        scratch_shapes=[pltpu.VMEM((heads, hd, hd), _F32), pltpu.VMEM((heads, chunk, chunk), _F32),
                        pltpu.VMEM((3, heads, chunk, hd), _F32), pltpu.VMEM((heads, 1, hd), _F32)],
        compiler_params=_params("arbitrary"),
        name="ret_out",
    )(logit_f, logit_b, zr, zr, zf, zf, b_states, head_gain)


def _window_sum(p, window):
    rows = p.shape[0]
    assert window >= 2 and window & (window - 1) == 0
    shifted = lambda a, k: pltpu.roll(a, k % rows, axis=0)
    s = shifted(p, 1) + p
    width = 2
    while width < window:
        s = shifted(s, width // 2) + shifted(s, -(width // 2))
        width *= 2
    return s


def _pool_kernel(prev_ref, cur_ref, next_ref, w_ref, scale_ref, o_ref, pad_ref, *, tm, seq, group_dim):
    i = pl.program_id(0)
    n_blocks = pl.num_programs(0)
    halo = POOL_HALO
    pad_ref[0:halo, :] = jnp.where(i > 0, prev_ref[...], 0.0)
    pad_ref[halo:halo + tm, :] = cur_ref[...]
    pad_ref[halo + tm:, :] = jnp.where(i < n_blocks - 1, next_ref[...], 0.0)

    row = i * tm + lax.broadcasted_iota(jnp.int32, (tm, 1), 0)
    for gi, window in enumerate(POOL_WINDOWS):
        lo = window // 2
        hi = window - 1 - lo
        cols = slice(gi * group_dim, (gi + 1) * group_dim)
        total = _window_sum(pad_ref[:, cols], window)[halo:halo + tm]
        count = (jnp.minimum(row + hi + 1, seq) - jnp.maximum(row - lo, 0)).astype(_F32)
        pooled = total / count - cur_ref[:, cols]
        mapped = jnp.dot(pooled.astype(_BF16), w_ref[gi], preferred_element_type=_F32)
        o_ref[:, cols] = (mapped * scale_ref[:, cols]).astype(o_ref.dtype)


def _pool(z, pool_w, pool_scale, *, tm=512):
    s, n = z.shape
    groups, group_dim, _ = pool_w.shape
    pool_dim = groups * group_dim
    assert groups == len(POOL_WINDOWS) and max(POOL_WINDOWS) // 2 <= POOL_HALO
    assert s % tm == 0 and tm % POOL_HALO == 0 and n % pool_dim == 0
    col = n // pool_dim - 1
    r = tm // POOL_HALO
    last = s // POOL_HALO - 1
    return pl.pallas_call(
        functools.partial(_pool_kernel, tm=tm, seq=s, group_dim=group_dim),
        grid=(s // tm,),
        in_specs=[
            pl.BlockSpec((POOL_HALO, pool_dim), lambda i: (jnp.maximum(i * r - 1, 0), col)),
            pl.BlockSpec((tm, pool_dim), lambda i: (i, col)),
            pl.BlockSpec((POOL_HALO, pool_dim), lambda i: (jnp.minimum((i + 1) * r, last), col)),
            pl.BlockSpec((groups, group_dim, group_dim), lambda i: (0, 0, 0)),
            pl.BlockSpec((1, pool_dim), lambda i: (0, 0)),
        ],
        out_specs=pl.BlockSpec((tm, pool_dim), lambda i: (i, 0)),
        out_shape=jax.ShapeDtypeStruct((s, pool_dim), _BF16),
        scratch_shapes=[pltpu.VMEM((tm + 2 * POOL_HALO, pool_dim), _F32)],
        compiler_params=_params("parallel"),
        name="pool",
    )(z, z, z, pool_w, pool_scale)


def _proj_out_kernel(x_ref, ret_ref, pool_ref, wr_ref, wp_ref, o_ref):
    acc = jnp.dot(ret_ref[...], wr_ref[...], preferred_element_type=_F32)
    acc += jnp.dot(pool_ref[...], wp_ref[...], preferred_element_type=_F32)
    o_ref[...] = x_ref[...] + acc


def _proj_out(x, ret, pool, w, *, tm=1024, tn=1024):
    s, d = x.shape
    kr = ret.shape[1]
    kp = pool.shape[1]
    assert kr == kp and w.shape == (kr + kp, d) and s % tm == 0 and d % tn == 0
    return pl.pallas_call(
        _proj_out_kernel,
        grid=(s // tm, d // tn),
        in_specs=[
            pl.BlockSpec((tm, tn), lambda i, j: (i, j)),
            pl.BlockSpec((tm, kr), lambda i, j: (i, 0)),
            pl.BlockSpec((tm, kp), lambda i, j: (i, 0)),
            pl.BlockSpec((kr, tn), lambda i, j: (0, j)),
            pl.BlockSpec((kp, tn), lambda i, j: (1, j)),
        ],
        out_specs=pl.BlockSpec((tm, tn), lambda i, j: (i, j)),
        out_shape=jax.ShapeDtypeStruct((s, d), _F32),
        compiler_params=_params("parallel", "arbitrary"),
        name="proj_out",
    )(x, ret, pool, w, w)


def kernel(x, positions, ffn1_norm, ffn1_w_gate, ffn1_w_up, ffn1_w_down, mix_norm, w_in,
           ret_decay_fwd, ret_decay_bwd, ret_head_norm, pool_w, pool_scale, w_out,
           ffn2_norm, ffn2_w_gate, ffn2_w_up, ffn2_w_down, final_norm):
    b, s, d = x.shape
    assert b == 1, "retention scan is written for a single sequence"
    depth = ffn1_norm.shape[0]
    heads = ret_decay_fwd.shape[1]
    ret_dim = ret_head_norm.shape[1]
    head_dim = ret_dim // heads

    freqs = 1.0 / (ROPE_BASE ** (jnp.arange(0, head_dim, 2, dtype=_F32) / head_dim))
    freqs = freqs.reshape(1, head_dim // 2)
    pos = positions.reshape(s, 1)
    final_gain = final_norm.reshape(1, d)
    row = lambda v: v.reshape(1, -1).astype(_F32)
    bf = lambda w: w.astype(_BF16)

    xs = x.reshape(s, d)
    for layer in range(depth):
        last = layer == depth - 1
        xs = _ffn(xs, row(ffn1_norm[layer]), ffn1_w_gate[layer], ffn1_w_up[layer],
                  ffn1_w_down[layer], final_gain, final_norm=False)
        zr, zf = _proj_in(xs, row(mix_norm[layer]), pos, freqs, bf(w_in[layer]),
                          ret_dim=ret_dim, head_dim=head_dim)
        ret = _retention(zr, zf, ret_decay_fwd[layer].reshape(heads, 1, 1).astype(_F32),
                         ret_decay_bwd[layer].reshape(heads, 1, 1).astype(_F32),
                         row(ret_head_norm[layer]), heads=heads, head_dim=head_dim)
        pool = _pool(zf, bf(pool_w[layer]), row(pool_scale[layer]))
        xs = _proj_out(xs, ret, pool, bf(w_out[layer]))
        xs = _ffn(xs, row(ffn2_norm[layer]), ffn2_w_gate[layer], ffn2_w_up[layer],
                  ffn2_w_down[layer], final_gain, final_norm=last)
    return xs.reshape(b, s, d)
```

```python
import functools

import jax
import jax.numpy as jnp
from jax import lax
from jax.experimental import pallas as pl
from jax.experimental.pallas import tpu as pltpu

EPS = 1e-6
ROPE_BASE = 10000.0
POOL_WINDOWS = (2, 4, 8, 16)
POOL_HALO = 16

_F32 = jnp.float32
_BF16 = jnp.bfloat16
_VMEM_LIMIT_BYTES = 62 * 1024 * 1024


def _params(*semantics):
    return pltpu.CompilerParams(dimension_semantics=semantics, vmem_limit_bytes=_VMEM_LIMIT_BYTES)


def _rmsnorm(x, gain):
    return x * lax.rsqrt(jnp.mean(x * x, axis=-1, keepdims=True) + EPS) * gain


def _silu(g):
    return g * jax.nn.sigmoid(g)


def _ffn_kernel(x_hbm, gain_ref, wg_ref, wu_ref, wd_ref, fgain_ref, *rest,
                n_casts, n_steps, n_row_blocks, tf, tail_subs, final_norm, row_chunk, col_chunk):
    cast_in = rest[:n_casts]
    o_ref = rest[n_casts]
    cast_out = rest[n_casts + 1:2 * n_casts + 1]
    x_buf, h_ref, x_sem = rest[2 * n_casts + 1:]
    i = pl.program_id(0)
    j = pl.program_id(1)
    tm, d = o_ref.shape

    def x_copy(block):
        return pltpu.make_async_copy(x_hbm.at[pl.ds(block * tm, tm), :], x_buf, x_sem)

    def activation(k):
        cols = slice(k * tf, (k + 1) * tf)
        h = h_ref[...]
        g = jnp.dot(h, wg_ref[:, cols], preferred_element_type=_F32)
        u = jnp.dot(h, wu_ref[:, cols], preferred_element_type=_F32)
        return (_silu(g) * u * 0.5).astype(_BF16)

    def step(n_subs):
        acts = [activation(k) for k in range(n_subs)]
        for src, dst in zip(cast_in, cast_out):
            dst[...] = src[...].astype(dst.dtype)
        for k, a in enumerate(acts):
            for c in range(0, d, col_chunk):
                cols = slice(c, c + col_chunk)
                o_ref[:, cols] += jnp.dot(a, wd_ref[k * tf:(k + 1) * tf, cols],
                                          preferred_element_type=_F32)

    @pl.when(j == 0)
    def _():
        @pl.when(i == 0)
        def _():
            x_copy(0).start()

        x_copy(i).wait()
        for r in range(0, tm, row_chunk):
            rows = slice(r, r + row_chunk)
            xr = x_buf[rows, :]
            h_ref[rows, :] = _rmsnorm(xr, gain_ref[...]).astype(_BF16)
            o_ref[rows, :] = xr
        step(2)

    @pl.when(j == 1)
    def _():
        @pl.when(i + 1 < n_row_blocks)
        def _():
            x_copy(i + 1).start()

    @pl.when(jnp.logical_and(j > 0, j < n_steps - 1))
    def _():
        step(2)

    @pl.when(j == n_steps - 1)
    def _():
        step(tail_subs)
        if final_norm:
            for r in range(0, tm, row_chunk):
                rows = slice(r, r + row_chunk)
                o_ref[rows, :] = _rmsnorm(o_ref[rows, :], fgain_ref[...])


_CAST_ROWS = 16


def _cast_specs(w, n_steps, total_steps):
    rows, width = w.shape
    assert rows % _CAST_ROWS == 0
    splits = [k for k in (4, 2, 1)
              if width % (k * 128) == 0 and (rows // _CAST_ROWS) * k <= total_steps]
    if not splits:
        return None
    k = splits[0]
    n_blocks = (rows // _CAST_ROWS) * k

    def index(i, j):
        b = jnp.minimum(i * n_steps + j, n_blocks - 1)
        return b // k, b % k

    block = (_CAST_ROWS, width // k)
    return pl.BlockSpec(block, index), pl.BlockSpec(block, index), jax.ShapeDtypeStruct(w.shape, _BF16)


def _ffn(x, gain, wg, wu, wd, fgain, *, final_norm, casts=(), tm=512, tf=256):
    s, d = x.shape
    f = wg.shape[1]
    assert s % tm == 0 and f % tf == 0
    n_steps = pl.cdiv(f, 2 * tf)
    assert n_steps >= 3
    tail_subs = (f - (n_steps - 1) * 2 * tf) // tf
    n_row_blocks = s // tm
    all_specs = [_cast_specs(w, n_steps, n_row_blocks * n_steps) for w in casts]
    hosted = [w for w, spec in zip(casts, all_specs) if spec is not None]
    cast_specs = [spec for spec in all_specs if spec is not None]
    kern = functools.partial(
        _ffn_kernel, n_casts=len(hosted), n_steps=n_steps, n_row_blocks=n_row_blocks, tf=tf,
        tail_subs=tail_subs, final_norm=final_norm, row_chunk=128, col_chunk=1024)
    out, *hosted_bf = pl.pallas_call(
        kern,
        grid=(n_row_blocks, n_steps),
        in_specs=[
            pl.BlockSpec(memory_space=pl.ANY),
            pl.BlockSpec((1, d), lambda i, j: (0, 0)),
            pl.BlockSpec((d, 2 * tf), lambda i, j: (0, j)),
            pl.BlockSpec((d, 2 * tf), lambda i, j: (0, j)),
            pl.BlockSpec((2 * tf, d), lambda i, j: (j, 0)),
            pl.BlockSpec((1, d), lambda i, j: (0, 0)),
        ] + [spec[0] for spec in cast_specs],
        out_specs=[pl.BlockSpec((tm, d), lambda i, j: (i, 0))] + [spec[1] for spec in cast_specs],
        out_shape=[jax.ShapeDtypeStruct((s, d), _F32)] + [spec[2] for spec in cast_specs],
        scratch_shapes=[pltpu.VMEM((tm, d), _F32), pltpu.VMEM((tm, d), _BF16),
                        pltpu.SemaphoreType.DMA],
        compiler_params=_params("arbitrary", "arbitrary"),
        name="ffn",
    )(x, gain, wg.astype(_BF16), wu.astype(_BF16), wd.astype(_BF16), fgain, *hosted)
    hosted_bf = iter(hosted_bf)
    return [out] + [next(hosted_bf) if spec is not None else w.astype(_BF16)
                    for w, spec in zip(casts, all_specs)]


def _proj_in_kernel(x_hbm, gain_ref, pos_ref, freq_ref, w_ref, zr_ref, zf_ref,
                    x_buf, h_ref, cos_ref, sin_ref, x_sem,
                    *, n_row_blocks, n_plain, n_q_blocks, heads_per_block, head_dim, k_scale, row_chunk):
    i = pl.program_id(0)
    j = pl.program_id(1)
    tm = x_buf.shape[0]
    half = head_dim // 2

    def x_copy(block):
        return pltpu.make_async_copy(x_hbm.at[pl.ds(block * tm, tm), :], x_buf, x_sem)

    def head_dot(t):
        cols = slice(t * head_dim, (t + 1) * head_dim)
        return jnp.dot(h_ref[...], w_ref[:, cols], preferred_element_type=_F32)

    def plain_step():
        for t in range(heads_per_block):
            zf_ref[:, t * head_dim:(t + 1) * head_dim] = head_dot(t)

    @pl.when(j == 0)
    def _():
        @pl.when(i == 0)
        def _():
            x_copy(0).start()

        x_copy(i).wait()
        for r in range(0, tm, row_chunk):
            rows = slice(r, r + row_chunk)
            h_ref[rows, :] = _rmsnorm(x_buf[rows, :], gain_ref[...]).astype(_BF16)
        ang = pos_ref[...].astype(_F32) * freq_ref[...]
        cos_ref[...] = jnp.cos(ang)
        sin_ref[...] = jnp.sin(ang)
        plain_step()

    @pl.when(j == 1)
    def _():
        @pl.when(i + 1 < n_row_blocks)
        def _():
            x_copy(i + 1).start()

    @pl.when(jnp.logical_and(j > 0, j < n_plain))
    def _():
        plain_step()

    @pl.when(j >= n_plain)
    def _():
        cos = cos_ref[...]
        sin = sin_ref[...]
        scale = jnp.where(j - n_plain >= n_q_blocks, k_scale, 1.0).astype(_F32)
        for t in range(heads_per_block):
            z = head_dot(t)
            t1 = z[:, :half]
            t2 = z[:, half:]
            lo = t * head_dim
            zr_ref[:, lo:lo + half] = ((t1 * cos - t2 * sin) * scale).astype(zr_ref.dtype)
            zr_ref[:, lo + half:lo + head_dim] = ((t1 * sin + t2 * cos) * scale).astype(zr_ref.dtype)


def _proj_in(x, gain, pos, freqs, w, *, ret_dim, head_dim, tm=1024, tn=1024):
    s, d = x.shape
    n = w.shape[1]
    assert s % tm == 0 and n % tn == 0 and ret_dim % tn == 0 and tn % head_dim == 0
    half = head_dim // 2
    n_rot = 2 * ret_dim // tn
    n_plain = n // tn - n_rot
    assert n_plain >= 2
    kern = functools.partial(
        _proj_in_kernel, n_row_blocks=s // tm, n_plain=n_plain, n_q_blocks=ret_dim // tn,
        heads_per_block=tn // head_dim, head_dim=head_dim, k_scale=head_dim ** -0.5, row_chunk=128)
    return pl.pallas_call(
        kern,
        grid=(s // tm, n // tn),
        in_specs=[
            pl.BlockSpec(memory_space=pl.ANY),
            pl.BlockSpec((1, d), lambda i, j: (0, 0)),
            pl.BlockSpec((tm, 1), lambda i, j: (i, 0)),
            pl.BlockSpec((1, half), lambda i, j: (0, 0)),
            pl.BlockSpec((d, tn), lambda i, j: (0, jnp.where(j < n_plain, j + n_rot, j - n_plain))),
        ],
        out_specs=[
            pl.BlockSpec((tm, tn), lambda i, j: (i, jnp.maximum(j - n_plain, 0))),
            pl.BlockSpec((tm, tn), lambda i, j: (i, jnp.minimum(j, n_plain - 1))),
        ],
        out_shape=[jax.ShapeDtypeStruct((s, n_rot * tn), _BF16),
                   jax.ShapeDtypeStruct((s, n_plain * tn), _F32)],
        scratch_shapes=[pltpu.VMEM((tm, d), _F32), pltpu.VMEM((tm, d), _BF16),
                        pltpu.VMEM((tm, half), _F32), pltpu.VMEM((tm, half), _F32),
                        pltpu.SemaphoreType.DMA],
        compiler_params=_params("arbitrary", "arbitrary"),
        name="proj_in",
    )(x, gain, pos, freqs, w)


def _contract_rows(a, b):
    return lax.dot_general(a, b, (((0,), (0,)), ((), ())), preferred_element_type=_F32)


def _ret_state_kernel(lgb_ref, k_ref, v_ref, b_ref, state_ref, vdec_ref, cdec_ref, *, chunk, heads, hd):
    t = pl.program_id(0)

    @pl.when(t == 0)
    def _():
        state_ref[...] = jnp.zeros_like(state_ref)
        idx = lax.broadcasted_iota(jnp.int32, (chunk, hd), 0).astype(_F32)
        for h in range(heads):
            lgb = jax.nn.log_sigmoid(lgb_ref[h])
            vdec_ref[h] = jnp.exp(idx * lgb)
            cdec_ref[h] = jnp.broadcast_to(jnp.exp(chunk * lgb), (1, hd))

    for h in range(heads):
        cols = slice(h * hd, (h + 1) * hd)
        b_ref[h, 0] = state_ref[h].astype(_BF16)
        v_dec = (v_ref[:, cols] * vdec_ref[h]).astype(_BF16)
        state_ref[h] = state_ref[h] * cdec_ref[h] + _contract_rows(k_ref[:, cols], v_dec)


def _ret_out_kernel(lgf_ref, lgb_ref, q_ref, k_ref, v_ref, g_ref, b_ref, hn_ref, o_ref,
                    state_ref, dm_ref, dec_ref, cdec_ref, *, chunk, heads, hd):
    c = pl.program_id(0)

    @pl.when(c == 0)
    def _():
        state_ref[...] = jnp.zeros_like(state_ref)
        rel = (lax.broadcasted_iota(jnp.int32, (chunk, chunk), 0)
               - lax.broadcasted_iota(jnp.int32, (chunk, chunk), 1)).astype(_F32)
        idx = lax.broadcasted_iota(jnp.int32, (chunk, hd), 0).astype(_F32)
        for h in range(heads):
            lgf = jax.nn.log_sigmoid(lgf_ref[h])
            lgb = jax.nn.log_sigmoid(lgb_ref[h])
            dm_ref[h] = jnp.exp(jnp.where(rel >= 0, rel * lgf, -rel * lgb))
            dec_ref[0, h] = jnp.exp((idx + 1.0) * lgf)
            dec_ref[1, h] = jnp.exp((chunk - idx) * lgb)
            dec_ref[2, h] = jnp.exp((chunk - 1.0 - idx) * lgf)
            cdec_ref[h] = jnp.broadcast_to(jnp.exp(chunk * lgf), (1, hd))

    for h in range(heads):
        cols = slice(h * hd, (h + 1) * hd)
        q = q_ref[:, cols]
        k = k_ref[:, cols]
        v = v_ref[:, cols]
        scores = lax.dot_general(q, k, (((1,), (1,)), ((), ())), preferred_element_type=_F32)
        scores = scores * dm_ref[h]
        out = jnp.dot(scores.astype(_BF16), v.astype(_BF16), preferred_element_type=_F32)
        out += jnp.dot(q, state_ref[h].astype(_BF16), preferred_element_type=_F32) * dec_ref[0, h]
        out += jnp.dot(q, b_ref[h, 0], preferred_element_type=_F32) * dec_ref[1, h]

        out = out * lax.rsqrt(jnp.mean(out * out, axis=-1, keepdims=True) + EPS)
        out = out * hn_ref[:, cols]
        o_ref[:, cols] = (out * _silu(g_ref[:, cols])).astype(o_ref.dtype)

        v_dec = (v * dec_ref[2, h]).astype(_BF16)
        state_ref[h] = state_ref[h] * cdec_ref[h] + _contract_rows(k, v_dec)


def _retention(zr, zf, logit_f, logit_b, head_gain, *, heads, head_dim, chunk=256):
    s = zr.shape[0]
    assert s % chunk == 0
    nc = s // chunk
    hd = head_dim
    rd = heads * hd
    decay_spec = pl.BlockSpec((heads, 1, 1), lambda c: (0, 0, 0))

    b_states = pl.pallas_call(
        functools.partial(_ret_state_kernel, chunk=chunk, heads=heads, hd=hd),
        grid=(nc,),
        in_specs=[
            decay_spec,
            pl.BlockSpec((chunk, rd), lambda t: (nc - 1 - t, 1)),
            pl.BlockSpec((chunk, rd), lambda t: (nc - 1 - t, 0)),
        ],
        out_specs=pl.BlockSpec((heads, 1, hd, hd), lambda t: (0, nc - 1 - t, 0, 0)),
        out_shape=jax.ShapeDtypeStruct((heads, nc, hd, hd), _BF16),
        scratch_shapes=[pltpu.VMEM((heads, hd, hd), _F32), pltpu.VMEM((heads, chunk, hd), _F32),
                        pltpu.VMEM((heads, 1, hd), _F32)],
        compiler_params=_params("arbitrary"),
        name="ret_state",
    )(logit_b, zr, zf)

    return pl.pallas_call(
        functools.partial(_ret_out_kernel, chunk=chunk, heads=heads, hd=hd),
        grid=(nc,),
        in_specs=[
            decay_spec,
            decay_spec,
            pl.BlockSpec((chunk, rd), lambda c: (c, 0)),
            pl.BlockSpec((chunk, rd), lambda c: (c, 1)),
            pl.BlockSpec((chunk, rd), lambda c: (c, 0)),
            pl.BlockSpec((chunk, rd), lambda c: (c, 1)),
            pl.BlockSpec((heads, 1, hd, hd), lambda c: (0, c, 0, 0)),
            pl.BlockSpec((1, rd), lambda c: (0, 0)),
        ],
        out_specs=pl.BlockSpec((chunk, rd), lambda c: (c, 0)),
        out_shape=jax.ShapeDtypeStruct((s, rd), _BF16),
        scratch_shapes=[pltpu.VMEM((heads, hd, hd), _F32), pltpu.VMEM((heads, chunk, chunk), _F32),
                        pltpu.VMEM((3, heads, chunk, hd), _F32), pltpu.VMEM((heads, 1, hd), _F32)],
        compiler_params=_params("arbitrary"),
        name="ret_out",
    )(logit_f, logit_b, zr, zr, zf, zf, b_states, head_gain)


def _window_sum(p, window):
    rows = p.shape[0]
    assert window >= 2 and window & (window - 1) == 0
    shifted = lambda a, k: pltpu.roll(a, k % rows, axis=0)
    s = shifted(p, 1) + p
    width = 2
    while width < window:
        s = shifted(s, width // 2) + shifted(s, -(width // 2))
        width *= 2
    return s


def _pool_kernel(prev_ref, cur_ref, next_ref, w_ref, scale_ref, o_ref, pad_ref, *, tm, seq, group_dim):
    i = pl.program_id(0)
    n_blocks = pl.num_programs(0)
    halo = POOL_HALO
    pad_ref[0:halo, :] = jnp.where(i > 0, prev_ref[...], 0.0)
    pad_ref[halo:halo + tm, :] = cur_ref[...]
    pad_ref[halo + tm:, :] = jnp.where(i < n_blocks - 1, next_ref[...], 0.0)

    row = i * tm + lax.broadcasted_iota(jnp.int32, (tm, 1), 0)
    for gi, window in enumerate(POOL_WINDOWS):
        lo = window // 2
        hi = window - 1 - lo
        cols = slice(gi * group_dim, (gi + 1) * group_dim)
        total = _window_sum(pad_ref[:, cols], window)[halo:halo + tm]
        count = (jnp.minimum(row + hi + 1, seq) - jnp.maximum(row - lo, 0)).astype(_F32)
        pooled = total / count - cur_ref[:, cols]
        mapped = jnp.dot(pooled.astype(_BF16), w_ref[gi], preferred_element_type=_F32)
        o_ref[:, cols] = (mapped * scale_ref[:, cols]).astype(o_ref.dtype)


def _pool(z, pool_w, pool_scale, *, tm=512):
    s, n = z.shape
    groups, group_dim, _ = pool_w.shape
    pool_dim = groups * group_dim
    assert groups == len(POOL_WINDOWS) and max(POOL_WINDOWS) // 2 <= POOL_HALO
    assert s % tm == 0 and tm % POOL_HALO == 0 and n % pool_dim == 0
    col = n // pool_dim - 1
    r = tm // POOL_HALO
    last = s // POOL_HALO - 1
    return pl.pallas_call(
        functools.partial(_pool_kernel, tm=tm, seq=s, group_dim=group_dim),
        grid=(s // tm,),
        in_specs=[
            pl.BlockSpec((POOL_HALO, pool_dim), lambda i: (jnp.maximum(i * r - 1, 0), col)),
            pl.BlockSpec((tm, pool_dim), lambda i: (i, col)),
            pl.BlockSpec((POOL_HALO, pool_dim), lambda i: (jnp.minimum((i + 1) * r, last), col)),
            pl.BlockSpec((groups, group_dim, group_dim), lambda i: (0, 0, 0)),
            pl.BlockSpec((1, pool_dim), lambda i: (0, 0)),
        ],
        out_specs=pl.BlockSpec((tm, pool_dim), lambda i: (i, 0)),
        out_shape=jax.ShapeDtypeStruct((s, pool_dim), _BF16),
        scratch_shapes=[pltpu.VMEM((tm + 2 * POOL_HALO, pool_dim), _F32)],
        compiler_params=_params("parallel"),
        name="pool",
    )(z, z, z, pool_w, pool_scale)


def _proj_out_kernel(x_ref, ret_ref, pool_ref, wr_ref, wp_ref, o_ref):
    acc = jnp.dot(ret_ref[...], wr_ref[...], preferred_element_type=_F32)
    acc += jnp.dot(pool_ref[...], wp_ref[...], preferred_element_type=_F32)
    o_ref[...] = x_ref[...] + acc


def _proj_out(x, ret, pool, w, *, tm=1024, tn=1024):
    s, d = x.shape
    kr = ret.shape[1]
    kp = pool.shape[1]
    assert kr == kp and w.shape == (kr + kp, d) and s % tm == 0 and d % tn == 0
    return pl.pallas_call(
        _proj_out_kernel,
        grid=(s // tm, d // tn),
        in_specs=[
            pl.BlockSpec((tm, tn), lambda i, j: (i, j)),
            pl.BlockSpec((tm, kr), lambda i, j: (i, 0)),
            pl.BlockSpec((tm, kp), lambda i, j: (i, 0)),
            pl.BlockSpec((kr, tn), lambda i, j: (0, j)),
            pl.BlockSpec((kp, tn), lambda i, j: (1, j)),
        ],
        out_specs=pl.BlockSpec((tm, tn), lambda i, j: (i, j)),
        out_shape=jax.ShapeDtypeStruct((s, d), _F32),
        compiler_params=_params("parallel", "arbitrary"),
        name="proj_out",
    )(x, ret, pool, w, w)


def kernel(x, positions, ffn1_norm, ffn1_w_gate, ffn1_w_up, ffn1_w_down, mix_norm, w_in,
           ret_decay_fwd, ret_decay_bwd, ret_head_norm, pool_w, pool_scale, w_out,
           ffn2_norm, ffn2_w_gate, ffn2_w_up, ffn2_w_down, final_norm):
    b, s, d = x.shape
    assert b == 1, "retention scan is written for a single sequence"
    depth = ffn1_norm.shape[0]
    heads = ret_decay_fwd.shape[1]
    ret_dim = ret_head_norm.shape[1]
    head_dim = ret_dim // heads

    freqs = 1.0 / (ROPE_BASE ** (jnp.arange(0, head_dim, 2, dtype=_F32) / head_dim))
    freqs = freqs.reshape(1, head_dim // 2)
    pos = positions.reshape(s, 1)
    final_gain = final_norm.reshape(1, d)
    row = lambda v: v.reshape(1, -1).astype(_F32)
    bf = lambda w: w.astype(_BF16)

    xs = x.reshape(s, d)
    for layer in range(depth):
        last = layer == depth - 1
        xs, w2_gate, w2_up, w2_down, w_in_bf, w_out_bf = _ffn(
            xs, row(ffn1_norm[layer]), ffn1_w_gate[layer], ffn1_w_up[layer], ffn1_w_down[layer],
            final_gain, final_norm=False,
            casts=(ffn2_w_gate[layer], ffn2_w_up[layer], ffn2_w_down[layer], w_in[layer], w_out[layer]))
        zr, zf = _proj_in(xs, row(mix_norm[layer]), pos, freqs, w_in_bf,
                          ret_dim=ret_dim, head_dim=head_dim)
        ret = _retention(zr, zf, ret_decay_fwd[layer].reshape(heads, 1, 1).astype(_F32),
                         ret_decay_bwd[layer].reshape(heads, 1, 1).astype(_F32),
                         row(ret_head_norm[layer]), heads=heads, head_dim=head_dim)
        pool = _pool(zf, bf(pool_w[layer]), row(pool_scale[layer]))
        xs = _proj_out(xs, ret, pool, w_out_bf)
        xs, = _ffn(xs, row(ffn2_norm[layer]), w2_gate, w2_up, w2_down, final_gain, final_norm=last)
    return xs.reshape(b, s, d)
```

```python
import functools

import jax
import jax.numpy as jnp
from jax import lax
from jax.experimental import pallas as pl
from jax.experimental.pallas import tpu as pltpu

EPS = 1e-6
ROPE_BASE = 10000.0
POOL_WINDOWS = (2, 4, 8, 16)
POOL_HALO = 16

_F32 = jnp.float32
_BF16 = jnp.bfloat16
_VMEM_LIMIT_BYTES = 62 * 1024 * 1024


def _params(*semantics):
    return pltpu.CompilerParams(dimension_semantics=semantics, vmem_limit_bytes=_VMEM_LIMIT_BYTES)


def _rmsnorm(x, gain):
    return x * lax.rsqrt(jnp.mean(x * x, axis=-1, keepdims=True) + EPS) * gain


def _silu(g):
    return g * jax.nn.sigmoid(g)


def _ffn_head_kernel(x_hbm, gain_ref, wg_ref, wu_ref, wd_ref, o_hbm, wg_bf, wu_bf, wd_bf,
                     acc_ref, h_ref, sem, *, n_steps, row_chunk, col_chunk):
    j = pl.program_id(0)
    tm, d = acc_ref.shape

    @pl.when(j == 0)
    def _():
        copy = pltpu.make_async_copy(x_hbm.at[pl.ds(0, tm), :], acc_ref, sem)
        copy.start()
        copy.wait()
        for r in range(0, tm, row_chunk):
            rows = slice(r, r + row_chunk)
            h_ref[rows, :] = _rmsnorm(acc_ref[rows, :], gain_ref[...]).astype(_BF16)

    wg_bf[...] = wg_ref[...].astype(_BF16)
    wu_bf[...] = wu_ref[...].astype(_BF16)
    wd_bf[...] = wd_ref[...].astype(_BF16)
    h = h_ref[...]
    g = jnp.dot(h, wg_bf[...], preferred_element_type=_F32)
    u = jnp.dot(h, wu_bf[...], preferred_element_type=_F32)
    a = (_silu(g) * u * 0.5).astype(_BF16)
    for c in range(0, d, col_chunk):
        cols = slice(c, c + col_chunk)
        acc_ref[:, cols] += jnp.dot(a, wd_bf[:, cols], preferred_element_type=_F32)

    @pl.when(j == n_steps - 1)
    def _():
        copy = pltpu.make_async_copy(acc_ref, o_hbm, sem)
        copy.start()
        copy.wait()


def _ffn_head(x, gain, wg, wu, wd, *, tm, tf):
    s, d = x.shape
    f = wg.shape[1]
    assert f % tf == 0
    n_steps = f // tf
    return pl.pallas_call(
        functools.partial(_ffn_head_kernel, n_steps=n_steps, row_chunk=128, col_chunk=1024),
        grid=(n_steps,),
        in_specs=[
            pl.BlockSpec(memory_space=pl.ANY),
            pl.BlockSpec((1, d), lambda j: (0, 0)),
            pl.BlockSpec((d, tf), lambda j: (0, j)),
            pl.BlockSpec((d, tf), lambda j: (0, j)),
            pl.BlockSpec((tf, d), lambda j: (j, 0)),
        ],
        out_specs=[
            pl.BlockSpec(memory_space=pl.ANY),
            pl.BlockSpec((d, tf), lambda j: (0, j)),
            pl.BlockSpec((d, tf), lambda j: (0, j)),
            pl.BlockSpec((tf, d), lambda j: (j, 0)),
        ],
        out_shape=[jax.ShapeDtypeStruct((tm, d), _F32), jax.ShapeDtypeStruct(wg.shape, _BF16),
                   jax.ShapeDtypeStruct(wu.shape, _BF16), jax.ShapeDtypeStruct(wd.shape, _BF16)],
        scratch_shapes=[pltpu.VMEM((tm, d), _F32), pltpu.VMEM((tm, d), _BF16), pltpu.SemaphoreType.DMA],
        compiler_params=_params("arbitrary"),
        name="ffn_head",
    )(x, gain, wg, wu, wd)


def _ffn_kernel(x_hbm, gain_ref, wg_ref, wu_ref, wd_ref, fgain_ref, *rest, has_head,
                n_casts, n_steps, n_row_blocks, tf, tail_subs, final_norm, row_chunk, col_chunk):
    if has_head:
        head_hbm, rest = rest[0], rest[1:]
    cast_in = rest[:n_casts]
    o_ref = rest[n_casts]
    cast_out = rest[n_casts + 1:2 * n_casts + 1]
    x_buf, h_ref, x_sem = rest[2 * n_casts + 1:]
    i = pl.program_id(0)
    j = pl.program_id(1)
    tm, d = o_ref.shape
    first = 1 if has_head else 0
    computed = i >= first

    def x_copy(block):
        return pltpu.make_async_copy(x_hbm.at[pl.ds(block * tm, tm), :], x_buf, x_sem)

    def activation(k):
        cols = slice(k * tf, (k + 1) * tf)
        h = h_ref[...]
        g = jnp.dot(h, wg_ref[:, cols], preferred_element_type=_F32)
        u = jnp.dot(h, wu_ref[:, cols], preferred_element_type=_F32)
        return (_silu(g) * u * 0.5).astype(_BF16)

    def cast_blocks():
        for src, dst in zip(cast_in, cast_out):
            dst[...] = src[...].astype(dst.dtype)

    def step(n_subs):
        acts = [activation(k) for k in range(n_subs)]
        cast_blocks()
        for k, a in enumerate(acts):
            for c in range(0, d, col_chunk):
                cols = slice(c, c + col_chunk)
                o_ref[:, cols] += jnp.dot(a, wd_ref[k * tf:(k + 1) * tf, cols],
                                          preferred_element_type=_F32)

    if has_head:
        @pl.when(i == 0)
        def _():
            @pl.when(j == 0)
            def _():
                copy = pltpu.make_async_copy(head_hbm, x_buf, x_sem)
                copy.start()
                copy.wait()
                for r in range(0, tm, row_chunk):
                    rows = slice(r, r + row_chunk)
                    o_ref[rows, :] = x_buf[rows, :]

            cast_blocks()

    @pl.when(jnp.logical_and(j == 0, computed))
    def _():
        if not has_head:
            @pl.when(i == 0)
            def _():
                x_copy(0).start()

        x_copy(i).wait()
        for r in range(0, tm, row_chunk):
            rows = slice(r, r + row_chunk)
            xr = x_buf[rows, :]
            h_ref[rows, :] = _rmsnorm(xr, gain_ref[...]).astype(_BF16)
            o_ref[rows, :] = xr
        step(2)

    @pl.when(j == 1)
    def _():
        @pl.when(i + 1 < n_row_blocks)
        def _():
            x_copy(i + 1).start()

    @pl.when(jnp.logical_and(jnp.logical_and(j > 0, j < n_steps - 1), computed))
    def _():
        step(2)

    @pl.when(jnp.logical_and(j == n_steps - 1, computed))
    def _():
        step(tail_subs)
        if final_norm:
            for r in range(0, tm, row_chunk):
                rows = slice(r, r + row_chunk)
                o_ref[rows, :] = _rmsnorm(o_ref[rows, :], fgain_ref[...])


_CAST_ROWS = 16


def _cast_specs(w, n_steps, total_steps):
    rows, width = w.shape
    assert rows % _CAST_ROWS == 0
    splits = [k for k in (4, 2, 1)
              if width % (k * 128) == 0 and (rows // _CAST_ROWS) * k <= total_steps]
    if not splits:
        return None
    k = splits[0]
    n_blocks = (rows // _CAST_ROWS) * k

    def index(i, j):
        b = jnp.minimum(i * n_steps + j, n_blocks - 1)
        return b // k, b % k

    block = (_CAST_ROWS, width // k)
    return pl.BlockSpec(block, index), pl.BlockSpec(block, index), jax.ShapeDtypeStruct(w.shape, _BF16)


def _ffn(x, gain, wg, wu, wd, fgain, *, final_norm, casts=(), tm=512, tf=256):
    s, d = x.shape
    f = wg.shape[1]
    assert s % tm == 0 and f % tf == 0
    n_steps = pl.cdiv(f, 2 * tf)
    assert n_steps >= 3
    tail_subs = (f - (n_steps - 1) * 2 * tf) // tf
    n_row_blocks = s // tm
    has_head = wg.dtype != _BF16 and n_row_blocks >= 2
    head = ()
    if has_head:
        head_rows, wg, wu, wd = _ffn_head(x, gain, wg, wu, wd, tm=tm, tf=tf)
        head = (head_rows,)
    w_col = (lambda i, j: jnp.where(i == 0, 0, j)) if has_head else (lambda i, j: j)
    all_specs = [_cast_specs(w, n_steps, n_row_blocks * n_steps) for w in casts]
    hosted = [w for w, spec in zip(casts, all_specs) if spec is not None]
    cast_specs = [spec for spec in all_specs if spec is not None]
    kern = functools.partial(
        _ffn_kernel, has_head=has_head, n_casts=len(hosted), n_steps=n_steps, n_row_blocks=n_row_blocks,
        tf=tf, tail_subs=tail_subs, final_norm=final_norm, row_chunk=128, col_chunk=1024)
    out, *hosted_bf = pl.pallas_call(
        kern,
        grid=(n_row_blocks, n_steps),
        in_specs=[
            pl.BlockSpec(memory_space=pl.ANY),
            pl.BlockSpec((1, d), lambda i, j: (0, 0)),
            pl.BlockSpec((d, 2 * tf), lambda i, j: (0, w_col(i, j))),
            pl.BlockSpec((d, 2 * tf), lambda i, j: (0, w_col(i, j))),
            pl.BlockSpec((2 * tf, d), lambda i, j: (w_col(i, j), 0)),
            pl.BlockSpec((1, d), lambda i, j: (0, 0)),
        ] + [pl.BlockSpec(memory_space=pl.ANY)] * len(head) + [spec[0] for spec in cast_specs],
        out_specs=[pl.BlockSpec((tm, d), lambda i, j: (i, 0))] + [spec[1] for spec in cast_specs],
        out_shape=[jax.ShapeDtypeStruct((s, d), _F32)] + [spec[2] for spec in cast_specs],
        scratch_shapes=[pltpu.VMEM((tm, d), _F32), pltpu.VMEM((tm, d), _BF16),
                        pltpu.SemaphoreType.DMA],
        compiler_params=_params("arbitrary", "arbitrary"),
        name="ffn",
    )(x, gain, wg.astype(_BF16), wu.astype(_BF16), wd.astype(_BF16), fgain, *head, *hosted)
    hosted_bf = iter(hosted_bf)
    return [out] + [next(hosted_bf) if spec is not None else w.astype(_BF16)
                    for w, spec in zip(casts, all_specs)]


def _proj_in_kernel(x_hbm, gain_ref, pos_ref, freq_ref, w_ref, zr_ref, zf_ref,
                    x_buf, h_ref, cos_ref, sin_ref, x_sem,
                    *, n_row_blocks, n_plain, n_q_blocks, heads_per_block, head_dim, k_scale, row_chunk):
    i = pl.program_id(0)
    j = pl.program_id(1)
    tm = x_buf.shape[0]
    half = head_dim // 2

    def x_copy(block):
        return pltpu.make_async_copy(x_hbm.at[pl.ds(block * tm, tm), :], x_buf, x_sem)

    def head_dot(t):
        cols = slice(t * head_dim, (t + 1) * head_dim)
        return jnp.dot(h_ref[...], w_ref[:, cols], preferred_element_type=_F32)

    def plain_step():
        for t in range(heads_per_block):
            zf_ref[:, t * head_dim:(t + 1) * head_dim] = head_dot(t)

    @pl.when(j == 0)
    def _():
        @pl.when(i == 0)
        def _():
            x_copy(0).start()

        x_copy(i).wait()
        for r in range(0, tm, row_chunk):
            rows = slice(r, r + row_chunk)
            h_ref[rows, :] = _rmsnorm(x_buf[rows, :], gain_ref[...]).astype(_BF16)
        ang = pos_ref[...].astype(_F32) * freq_ref[...]
        cos_ref[...] = jnp.cos(ang)
        sin_ref[...] = jnp.sin(ang)
        plain_step()

    @pl.when(j == 1)
    def _():
        @pl.when(i + 1 < n_row_blocks)
        def _():
            x_copy(i + 1).start()

    @pl.when(jnp.logical_and(j > 0, j < n_plain))
    def _():
        plain_step()

    @pl.when(j >= n_plain)
    def _():
        cos = cos_ref[...]
        sin = sin_ref[...]
        scale = jnp.where(j - n_plain >= n_q_blocks, k_scale, 1.0).astype(_F32)
        for t in range(heads_per_block):
            z = head_dot(t)
            t1 = z[:, :half]
            t2 = z[:, half:]
            lo = t * head_dim
            zr_ref[:, lo:lo + half] = ((t1 * cos - t2 * sin) * scale).astype(zr_ref.dtype)
            zr_ref[:, lo + half:lo + head_dim] = ((t1 * sin + t2 * cos) * scale).astype(zr_ref.dtype)


def _proj_in(x, gain, pos, freqs, w, *, ret_dim, head_dim, tm=1024, tn=1024):
    s, d = x.shape
    n = w.shape[1]
    assert s % tm == 0 and n % tn == 0 and ret_dim % tn == 0 and tn % head_dim == 0
    half = head_dim // 2
    n_rot = 2 * ret_dim // tn
    n_plain = n // tn - n_rot
    assert n_plain >= 2
    kern = functools.partial(
        _proj_in_kernel, n_row_blocks=s // tm, n_plain=n_plain, n_q_blocks=ret_dim // tn,
        heads_per_block=tn // head_dim, head_dim=head_dim, k_scale=head_dim ** -0.5, row_chunk=128)
    return pl.pallas_call(
        kern,
        grid=(s // tm, n // tn),
        in_specs=[
            pl.BlockSpec(memory_space=pl.ANY),
            pl.BlockSpec((1, d), lambda i, j: (0, 0)),
            pl.BlockSpec((tm, 1), lambda i, j: (i, 0)),
            pl.BlockSpec((1, half), lambda i, j: (0, 0)),
            pl.BlockSpec((d, tn), lambda i, j: (0, jnp.where(j < n_plain, j + n_rot, j - n_plain))),
        ],
        out_specs=[
            pl.BlockSpec((tm, tn), lambda i, j: (i, jnp.maximum(j - n_plain, 0))),
            pl.BlockSpec((tm, tn), lambda i, j: (i, jnp.minimum(j, n_plain - 1))),
        ],
        out_shape=[jax.ShapeDtypeStruct((s, n_rot * tn), _BF16),
                   jax.ShapeDtypeStruct((s, n_plain * tn), _F32)],
        scratch_shapes=[pltpu.VMEM((tm, d), _F32), pltpu.VMEM((tm, d), _BF16),
                        pltpu.VMEM((tm, half), _F32), pltpu.VMEM((tm, half), _F32),
                        pltpu.SemaphoreType.DMA],
        compiler_params=_params("arbitrary", "arbitrary"),
        name="proj_in",
    )(x, gain, pos, freqs, w)


def _contract_rows(a, b):
    return lax.dot_general(a, b, (((0,), (0,)), ((), ())), preferred_element_type=_F32)


def _ret_state_kernel(lgb_ref, k_ref, v_ref, b_ref, state_ref, vdec_ref, cdec_ref, *, chunk, heads, hd):
    t = pl.program_id(0)

    @pl.when(t == 0)
    def _():
        state_ref[...] = jnp.zeros_like(state_ref)
        idx = lax.broadcasted_iota(jnp.int32, (chunk, hd), 0).astype(_F32)
        for h in range(heads):
            lgb = jax.nn.log_sigmoid(lgb_ref[h])
            vdec_ref[h] = jnp.exp(idx * lgb)
            cdec_ref[h] = jnp.broadcast_to(jnp.exp(chunk * lgb), (1, hd))

    for h in range(heads):
        cols = slice(h * hd, (h + 1) * hd)
        b_ref[h, 0] = state_ref[h].astype(_BF16)
        v_dec = (v_ref[:, cols] * vdec_ref[h]).astype(_BF16)
        state_ref[h] = state_ref[h] * cdec_ref[h] + _contract_rows(k_ref[:, cols], v_dec)


def _ret_out_kernel(lgf_ref, lgb_ref, q_ref, k_ref, v_ref, g_ref, b_ref, hn_ref, o_ref,
                    state_ref, dm_ref, dec_ref, cdec_ref, *, chunk, heads, hd):
    c = pl.program_id(0)

    @pl.when(c == 0)
    def _():
        state_ref[...] = jnp.zeros_like(state_ref)
        rel = (lax.broadcasted_iota(jnp.int32, (chunk, chunk), 0)
               - lax.broadcasted_iota(jnp.int32, (chunk, chunk), 1)).astype(_F32)
        idx = lax.broadcasted_iota(jnp.int32, (chunk, hd), 0).astype(_F32)
        for h in range(heads):
            lgf = jax.nn.log_sigmoid(lgf_ref[h])
            lgb = jax.nn.log_sigmoid(lgb_ref[h])
            dm_ref[h] = jnp.exp(jnp.where(rel >= 0, rel * lgf, -rel * lgb))
            dec_ref[0, h] = jnp.exp((idx + 1.0) * lgf)
            dec_ref[1, h] = jnp.exp((chunk - idx) * lgb)
            dec_ref[2, h] = jnp.exp((chunk - 1.0 - idx) * lgf)
            cdec_ref[h] = jnp.broadcast_to(jnp.exp(chunk * lgf), (1, hd))

    for h in range(heads):
        cols = slice(h * hd, (h + 1) * hd)
        q = q_ref[:, cols]
        k = k_ref[:, cols]
        v = v_ref[:, cols]
        scores = lax.dot_general(q, k, (((1,), (1,)), ((), ())), preferred_element_type=_F32)
        scores = scores * dm_ref[h]
        out = jnp.dot(scores.astype(_BF16), v.astype(_BF16), preferred_element_type=_F32)
        out += jnp.dot(q, state_ref[h].astype(_BF16), preferred_element_type=_F32) * dec_ref[0, h]
        out += jnp.dot(q, b_ref[h, 0], preferred_element_type=_F32) * dec_ref[1, h]

        out = out * lax.rsqrt(jnp.mean(out * out, axis=-1, keepdims=True) + EPS)
        out = out * hn_ref[:, cols]
        o_ref[:, cols] = (out * _silu(g_ref[:, cols])).astype(o_ref.dtype)

        v_dec = (v * dec_ref[2, h]).astype(_BF16)
        state_ref[h] = state_ref[h] * cdec_ref[h] + _contract_rows(k, v_dec)


def _retention(zr, zf, logit_f, logit_b, head_gain, *, heads, head_dim, chunk=256):
    s = zr.shape[0]
    assert s % chunk == 0
    nc = s // chunk
    hd = head_dim
    rd = heads * hd
    decay_spec = pl.BlockSpec((heads, 1, 1), lambda c: (0, 0, 0))

    b_states = pl.pallas_call(
        functools.partial(_ret_state_kernel, chunk=chunk, heads=heads, hd=hd),
        grid=(nc,),
        in_specs=[
            decay_spec,
            pl.BlockSpec((chunk, rd), lambda t: (nc - 1 - t, 1)),
            pl.BlockSpec((chunk, rd), lambda t: (nc - 1 - t, 0)),
        ],
        out_specs=pl.BlockSpec((heads, 1, hd, hd), lambda t: (0, nc - 1 - t, 0, 0)),
        out_shape=jax.ShapeDtypeStruct((heads, nc, hd, hd), _BF16),
        scratch_shapes=[pltpu.VMEM((heads, hd, hd), _F32), pltpu.VMEM((heads, chunk, hd), _F32),
                        pltpu.VMEM((heads, 1, hd), _F32)],
        compiler_params=_params("arbitrary"),
        name="ret_state",
    )(logit_b, zr, zf)

    return pl.pallas_call(
        functools.partial(_ret_out_kernel, chunk=chunk, heads=heads, hd=hd),
        grid=(nc,),
        in_specs=[
            decay_spec,
            decay_spec,
            pl.BlockSpec((chunk, rd), lambda c: (c, 0)),
            pl.BlockSpec((chunk, rd), lambda c: (c, 1)),
            pl.BlockSpec((chunk, rd), lambda c: (c, 0)),
            pl.BlockSpec((chunk, rd), lambda c: (c, 1)),
            pl.BlockSpec((heads, 1, hd, hd), lambda c: (0, c, 0, 0)),
            pl.BlockSpec((1, rd), lambda c: (0, 0)),
        ],
        out_specs=pl.BlockSpec((chunk, rd), lambda c: (c, 0)),
        out_shape=jax.ShapeDtypeStruct((s, rd), _BF16),
        scratch_shapes=[pltpu.VMEM((heads, hd, hd), _F32), pltpu.VMEM((heads, chunk, chunk), _F32),
                        pltpu.VMEM((3, heads, chunk, hd), _F32), pltpu.VMEM((heads, 1, hd), _F32)],
        compiler_params=_params("arbitrary"),
        name="ret_out",
    )(logit_f, logit_b, zr, zr, zf, zf, b_states, head_gain)


def _window_sum(p, window):
    rows = p.shape[0]
    assert window >= 2 and window & (window - 1) == 0
    shifted = lambda a, k: pltpu.roll(a, k % rows, axis=0)
    s = shifted(p, 1) + p
    width = 2
    while width < window:
        s = shifted(s, width // 2) + shifted(s, -(width // 2))
        width *= 2
    return s


def _pool_kernel(prev_ref, cur_ref, next_ref, w_ref, scale_ref, o_ref, pad_ref, *, tm, seq, group_dim):
    i = pl.program_id(0)
    n_blocks = pl.num_programs(0)
    halo = POOL_HALO
    pad_ref[0:halo, :] = jnp.where(i > 0, prev_ref[...], 0.0)
    pad_ref[halo:halo + tm, :] = cur_ref[...]
    pad_ref[halo + tm:, :] = jnp.where(i < n_blocks - 1, next_ref[...], 0.0)

    row = i * tm + lax.broadcasted_iota(jnp.int32, (tm, 1), 0)
    for gi, window in enumerate(POOL_WINDOWS):
        lo = window // 2
        hi = window - 1 - lo
        cols = slice(gi * group_dim, (gi + 1) * group_dim)
        total = _window_sum(pad_ref[:, cols], window)[halo:halo + tm]
        count = (jnp.minimum(row + hi + 1, seq) - jnp.maximum(row - lo, 0)).astype(_F32)
        pooled = total / count - cur_ref[:, cols]
        mapped = jnp.dot(pooled.astype(_BF16), w_ref[gi], preferred_element_type=_F32)
        o_ref[:, cols] = (mapped * scale_ref[:, cols]).astype(o_ref.dtype)


def _pool(z, pool_w, pool_scale, *, tm=512):
    s, n = z.shape
    groups, group_dim, _ = pool_w.shape
    pool_dim = groups * group_dim
    assert groups == len(POOL_WINDOWS) and max(POOL_WINDOWS) // 2 <= POOL_HALO
    assert s % tm == 0 and tm % POOL_HALO == 0 and n % pool_dim == 0
    col = n // pool_dim - 1
    r = tm // POOL_HALO
    last = s // POOL_HALO - 1
    return pl.pallas_call(
        functools.partial(_pool_kernel, tm=tm, seq=s, group_dim=group_dim),
        grid=(s // tm,),
        in_specs=[
            pl.BlockSpec((POOL_HALO, pool_dim), lambda i: (jnp.maximum(i * r - 1, 0), col)),
            pl.BlockSpec((tm, pool_dim), lambda i: (i, col)),
            pl.BlockSpec((POOL_HALO, pool_dim), lambda i: (jnp.minimum((i + 1) * r, last), col)),
            pl.BlockSpec((groups, group_dim, group_dim), lambda i: (0, 0, 0)),
            pl.BlockSpec((1, pool_dim), lambda i: (0, 0)),
        ],
        out_specs=pl.BlockSpec((tm, pool_dim), lambda i: (i, 0)),
        out_shape=jax.ShapeDtypeStruct((s, pool_dim), _BF16),
        scratch_shapes=[pltpu.VMEM((tm + 2 * POOL_HALO, pool_dim), _F32)],
        compiler_params=_params("parallel"),
        name="pool",
    )(z, z, z, pool_w, pool_scale)


def _proj_out_kernel(x_ref, ret_ref, pool_ref, wr_ref, wp_ref, o_ref):
    acc = jnp.dot(ret_ref[...], wr_ref[...], preferred_element_type=_F32)
    acc += jnp.dot(pool_ref[...], wp_ref[...], preferred_element_type=_F32)
    o_ref[...] = x_ref[...] + acc


def _proj_out(x, ret, pool, w, *, tm=1024, tn=1024):
    s, d = x.shape
    kr = ret.shape[1]
    kp = pool.shape[1]
    assert kr == kp and w.shape == (kr + kp, d) and s % tm == 0 and d % tn == 0
    return pl.pallas_call(
        _proj_out_kernel,
        grid=(s // tm, d // tn),
        in_specs=[
            pl.BlockSpec((tm, tn), lambda i, j: (i, j)),
            pl.BlockSpec((tm, kr), lambda i, j: (i, 0)),
            pl.BlockSpec((tm, kp), lambda i, j: (i, 0)),
            pl.BlockSpec((kr, tn), lambda i, j: (0, j)),
            pl.BlockSpec((kp, tn), lambda i, j: (1, j)),
        ],
        out_specs=pl.BlockSpec((tm, tn), lambda i, j: (i, j)),
        out_shape=jax.ShapeDtypeStruct((s, d), _F32),
        compiler_params=_params("parallel", "arbitrary"),
        name="proj_out",
    )(x, ret, pool, w, w)


def kernel(x, positions, ffn1_norm, ffn1_w_gate, ffn1_w_up, ffn1_w_down, mix_norm, w_in,
           ret_decay_fwd, ret_decay_bwd, ret_head_norm, pool_w, pool_scale, w_out,
           ffn2_norm, ffn2_w_gate, ffn2_w_up, ffn2_w_down, final_norm):
    b, s, d = x.shape
    assert b == 1, "retention scan is written for a single sequence"
    depth = ffn1_norm.shape[0]
    heads = ret_decay_fwd.shape[1]
    ret_dim = ret_head_norm.shape[1]
    head_dim = ret_dim // heads

    freqs = 1.0 / (ROPE_BASE ** (jnp.arange(0, head_dim, 2, dtype=_F32) / head_dim))
    freqs = freqs.reshape(1, head_dim // 2)
    pos = positions.reshape(s, 1)
    final_gain = final_norm.reshape(1, d)
    row = lambda v: v.reshape(1, -1).astype(_F32)
    bf = lambda w: w.astype(_BF16)

    xs = x.reshape(s, d)
    for layer in range(depth):
        last = layer == depth - 1
        xs, w2_gate, w2_up, w2_down, w_in_bf, w_out_bf = _ffn(
            xs, row(ffn1_norm[layer]), ffn1_w_gate[layer], ffn1_w_up[layer], ffn1_w_down[layer],
            final_gain, final_norm=False,
            casts=(ffn2_w_gate[layer], ffn2_w_up[layer], ffn2_w_down[layer], w_in[layer], w_out[layer]))
        zr, zf = _proj_in(xs, row(mix_norm[layer]), pos, freqs, w_in_bf,
                          ret_dim=ret_dim, head_dim=head_dim)
        ret = _retention(zr, zf, ret_decay_fwd[layer].reshape(heads, 1, 1).astype(_F32),
                         ret_decay_bwd[layer].reshape(heads, 1, 1).astype(_F32),
                         row(ret_head_norm[layer]), heads=heads, head_dim=head_dim)
        pool = _pool(zf, bf(pool_w[layer]), row(pool_scale[layer]))
        xs = _proj_out(xs, ret, pool, w_out_bf)
        xs, = _ffn(xs, row(ffn2_norm[layer]), w2_gate, w2_up, w2_down, final_gain, final_norm=last)
    return xs.reshape(b, s, d)
```

```python
import functools

import jax
import jax.numpy as jnp
from jax import lax
from jax.experimental import pallas as pl
from jax.experimental.pallas import tpu as pltpu

EPS = 1e-6
ROPE_BASE = 10000.0
POOL_WINDOWS = (2, 4, 8, 16)
POOL_HALO = 16

_F32 = jnp.float32
_BF16 = jnp.bfloat16
_VMEM_LIMIT_BYTES = 62 * 1024 * 1024


def _params(*semantics):
    return pltpu.CompilerParams(dimension_semantics=semantics, vmem_limit_bytes=_VMEM_LIMIT_BYTES)


def _rmsnorm(x, gain):
    return x * lax.rsqrt(jnp.mean(x * x, axis=-1, keepdims=True) + EPS) * gain


def _silu(g):
    return g * jax.nn.sigmoid(g)


def _ffn_head_kernel(x_hbm, gain_ref, wg_ref, wu_ref, wd_ref, o_hbm, wg_hbm, wu_hbm, wd_hbm,
                     acc_ref, h_ref, wg_bf, wu_bf, wd_bf, sem, w_sem,
                     *, n_steps, tf, row_chunk, row_part, col_chunk):
    j = pl.program_id(0)
    tm, d = acc_ref.shape

    def weight_copies(step):
        cols = pl.ds(pl.multiple_of(step * tf, tf), tf)
        return (pltpu.make_async_copy(wg_bf, wg_hbm.at[:, cols], w_sem.at[0]),
                pltpu.make_async_copy(wu_bf, wu_hbm.at[:, cols], w_sem.at[1]),
                pltpu.make_async_copy(wd_bf, wd_hbm.at[cols, :], w_sem.at[2]))

    @pl.when(j == 0)
    def _():
        copy = pltpu.make_async_copy(x_hbm.at[pl.ds(0, tm), :], acc_ref, sem)
        copy.start()
        copy.wait()
        for r in range(0, tm, row_chunk):
            rows = slice(r, r + row_chunk)
            h_ref[rows, :] = _rmsnorm(acc_ref[rows, :], gain_ref[...]).astype(_BF16)

    @pl.when(j > 0)
    def _():
        for copy in weight_copies(j - 1):
            copy.wait()

    wg_bf[...] = wg_ref[...].astype(_BF16)
    wu_bf[...] = wu_ref[...].astype(_BF16)
    wd_bf[...] = wd_ref[...].astype(_BF16)
    for copy in weight_copies(j):
        copy.start()

    parts = [slice(r, r + row_part) for r in range(0, tm, row_part)]
    acts = []
    for rows in parts:
        h = h_ref[rows, :]
        g = jnp.dot(h, wg_bf[...], preferred_element_type=_F32)
        u = jnp.dot(h, wu_bf[...], preferred_element_type=_F32)
        acts.append((_silu(g) * u * 0.5).astype(_BF16))
    for rows, a in zip(parts, acts):
        for c in range(0, d, col_chunk):
            cols = slice(c, c + col_chunk)
            acc_ref[rows, cols] += jnp.dot(a, wd_bf[:, cols], preferred_element_type=_F32)

    @pl.when(j == n_steps - 1)
    def _():
        for copy in weight_copies(j):
            copy.wait()
        copy = pltpu.make_async_copy(acc_ref, o_hbm, sem)
        copy.start()
        copy.wait()


def _ffn_head(x, gain, wg, wu, wd, *, rows, tf):
    s, d = x.shape
    f = wg.shape[1]
    assert f % tf == 0 and rows % 256 == 0
    n_steps = f // tf
    any_spec = pl.BlockSpec(memory_space=pl.ANY)
    return pl.pallas_call(
        functools.partial(_ffn_head_kernel, n_steps=n_steps, tf=tf, row_chunk=128, row_part=256,
                          col_chunk=1024),
        grid=(n_steps,),
        in_specs=[
            any_spec,
            pl.BlockSpec((1, d), lambda j: (0, 0)),
            pl.BlockSpec((d, tf), lambda j: (0, j)),
            pl.BlockSpec((d, tf), lambda j: (0, j)),
            pl.BlockSpec((tf, d), lambda j: (j, 0)),
        ],
        out_specs=[any_spec, any_spec, any_spec, any_spec],
        out_shape=[jax.ShapeDtypeStruct((rows, d), _F32), jax.ShapeDtypeStruct(wg.shape, _BF16),
                   jax.ShapeDtypeStruct(wu.shape, _BF16), jax.ShapeDtypeStruct(wd.shape, _BF16)],
        scratch_shapes=[pltpu.VMEM((rows, d), _F32), pltpu.VMEM((rows, d), _BF16),
                        pltpu.VMEM((d, tf), _BF16), pltpu.VMEM((d, tf), _BF16), pltpu.VMEM((tf, d), _BF16),
                        pltpu.SemaphoreType.DMA, pltpu.SemaphoreType.DMA((3,))],
        compiler_params=_params("arbitrary"),
        name="ffn_head",
    )(x, gain, wg, wu, wd)


def _ffn_kernel(x_hbm, gain_ref, wg_ref, wu_ref, wd_ref, fgain_ref, *rest, head_blocks,
                n_casts, n_steps, n_row_blocks, tf, tail_subs, final_norm, row_chunk, col_chunk):
    if head_blocks:
        head_hbm, rest = rest[0], rest[1:]
    cast_in = rest[:n_casts]
    o_ref = rest[n_casts]
    cast_out = rest[n_casts + 1:2 * n_casts + 1]
    x_buf, h_ref, x_sem = rest[2 * n_casts + 1:]
    i = pl.program_id(0)
    j = pl.program_id(1)
    tm, d = o_ref.shape
    computed = i >= head_blocks

    def x_copy(block):
        return pltpu.make_async_copy(x_hbm.at[pl.ds(block * tm, tm), :], x_buf, x_sem)

    def activation(k):
        cols = slice(k * tf, (k + 1) * tf)
        h = h_ref[...]
        g = jnp.dot(h, wg_ref[:, cols], preferred_element_type=_F32)
        u = jnp.dot(h, wu_ref[:, cols], preferred_element_type=_F32)
        return (_silu(g) * u * 0.5).astype(_BF16)

    def cast_blocks():
        for src, dst in zip(cast_in, cast_out):
            dst[...] = src[...].astype(dst.dtype)

    def step(n_subs):
        acts = [activation(k) for k in range(n_subs)]
        cast_blocks()
        for k, a in enumerate(acts):
            for c in range(0, d, col_chunk):
                cols = slice(c, c + col_chunk)
                o_ref[:, cols] += jnp.dot(a, wd_ref[k * tf:(k + 1) * tf, cols],
                                          preferred_element_type=_F32)

    if head_blocks:
        @pl.when(i < head_blocks)
        def _():
            @pl.when(j == 0)
            def _():
                copy = pltpu.make_async_copy(head_hbm.at[pl.ds(i * tm, tm), :], x_buf, x_sem)
                copy.start()
                copy.wait()
                for r in range(0, tm, row_chunk):
                    rows = slice(r, r + row_chunk)
                    o_ref[rows, :] = x_buf[rows, :]

            cast_blocks()

    @pl.when(jnp.logical_and(j == 0, computed))
    def _():
        if not head_blocks:
            @pl.when(i == 0)
            def _():
                x_copy(0).start()

        x_copy(i).wait()
        for r in range(0, tm, row_chunk):
            rows = slice(r, r + row_chunk)
            xr = x_buf[rows, :]
            h_ref[rows, :] = _rmsnorm(xr, gain_ref[...]).astype(_BF16)
            o_ref[rows, :] = xr
        step(2)

    @pl.when(j == 1)
    def _():
        @pl.when(jnp.logical_and(i + 1 < n_row_blocks, i + 1 >= head_blocks))
        def _():
            x_copy(i + 1).start()

    @pl.when(jnp.logical_and(jnp.logical_and(j > 0, j < n_steps - 1), computed))
    def _():
        step(2)

    @pl.when(jnp.logical_and(j == n_steps - 1, computed))
    def _():
        step(tail_subs)
        if final_norm:
            for r in range(0, tm, row_chunk):
                rows = slice(r, r + row_chunk)
                o_ref[rows, :] = _rmsnorm(o_ref[rows, :], fgain_ref[...])


_CAST_ROWS = 16


def _cast_specs(w, n_steps, total_steps):
    rows, width = w.shape
    assert rows % _CAST_ROWS == 0
    splits = [k for k in (4, 2, 1)
              if width % (k * 128) == 0 and (rows // _CAST_ROWS) * k <= total_steps]
    if not splits:
        return None
    k = splits[0]
    n_blocks = (rows // _CAST_ROWS) * k

    def index(i, j):
        b = jnp.minimum(i * n_steps + j, n_blocks - 1)
        return b // k, b % k

    block = (_CAST_ROWS, width // k)
    return pl.BlockSpec(block, index), pl.BlockSpec(block, index), jax.ShapeDtypeStruct(w.shape, _BF16)


def _ffn(x, gain, wg, wu, wd, fgain, *, final_norm, casts=(), tm=512, tf=256):
    s, d = x.shape
    f = wg.shape[1]
    assert s % tm == 0 and f % tf == 0
    n_steps = pl.cdiv(f, 2 * tf)
    assert n_steps >= 3
    tail_subs = (f - (n_steps - 1) * 2 * tf) // tf
    n_row_blocks = s // tm
    head_blocks = min(2, n_row_blocks - 1) if wg.dtype != _BF16 else 0
    head = ()
    if head_blocks:
        head_rows, wg, wu, wd = _ffn_head(x, gain, wg, wu, wd, rows=head_blocks * tm, tf=tf)
        head = (head_rows,)
    w_col = lambda i, j: jnp.where(i < head_blocks, 0, j)
    all_specs = [_cast_specs(w, n_steps, n_row_blocks * n_steps) for w in casts]
    hosted = [w for w, spec in zip(casts, all_specs) if spec is not None]
    cast_specs = [spec for spec in all_specs if spec is not None]
    kern = functools.partial(
        _ffn_kernel, head_blocks=head_blocks, n_casts=len(hosted), n_steps=n_steps, n_row_blocks=n_row_blocks,
        tf=tf, tail_subs=tail_subs, final_norm=final_norm, row_chunk=128, col_chunk=1024)
    out, *hosted_bf = pl.pallas_call(
        kern,
        grid=(n_row_blocks, n_steps),
        in_specs=[
            pl.BlockSpec(memory_space=pl.ANY),
            pl.BlockSpec((1, d), lambda i, j: (0, 0)),
            pl.BlockSpec((d, 2 * tf), lambda i, j: (0, w_col(i, j))),
            pl.BlockSpec((d, 2 * tf), lambda i, j: (0, w_col(i, j))),
            pl.BlockSpec((2 * tf, d), lambda i, j: (w_col(i, j), 0)),
            pl.BlockSpec((1, d), lambda i, j: (0, 0)),
        ] + [pl.BlockSpec(memory_space=pl.ANY)] * len(head) + [spec[0] for spec in cast_specs],
        out_specs=[pl.BlockSpec((tm, d), lambda i, j: (i, 0))] + [spec[1] for spec in cast_specs],
        out_shape=[jax.ShapeDtypeStruct((s, d), _F32)] + [spec[2] for spec in cast_specs],
        scratch_shapes=[pltpu.VMEM((tm, d), _F32), pltpu.VMEM((tm, d), _BF16),
                        pltpu.SemaphoreType.DMA],
        compiler_params=_params("arbitrary", "arbitrary"),
        name="ffn",
    )(x, gain, wg.astype(_BF16), wu.astype(_BF16), wd.astype(_BF16), fgain, *head, *hosted)
    hosted_bf = iter(hosted_bf)
    return [out] + [next(hosted_bf) if spec is not None else w.astype(_BF16)
                    for w, spec in zip(casts, all_specs)]


def _proj_in_kernel(x_hbm, gain_ref, pos_ref, freq_ref, w_ref, zr_ref, zf_ref,
                    x_buf, h_ref, cos_ref, sin_ref, x_sem,
                    *, n_row_blocks, n_plain, n_q_blocks, heads_per_block, head_dim, k_scale, row_chunk):
    i = pl.program_id(0)
    j = pl.program_id(1)
    tm = x_buf.shape[0]
    half = head_dim // 2

    def x_copy(block):
        return pltpu.make_async_copy(x_hbm.at[pl.ds(block * tm, tm), :], x_buf, x_sem)

    def head_dot(t):
        cols = slice(t * head_dim, (t + 1) * head_dim)
        return jnp.dot(h_ref[...], w_ref[:, cols], preferred_element_type=_F32)

    def plain_step():
        for t in range(heads_per_block):
            zf_ref[:, t * head_dim:(t + 1) * head_dim] = head_dot(t)

    @pl.when(j == 0)
    def _():
        @pl.when(i == 0)
        def _():
            x_copy(0).start()

        x_copy(i).wait()
        for r in range(0, tm, row_chunk):
            rows = slice(r, r + row_chunk)
            h_ref[rows, :] = _rmsnorm(x_buf[rows, :], gain_ref[...]).astype(_BF16)
        ang = pos_ref[...].astype(_F32) * freq_ref[...]
        cos_ref[...] = jnp.cos(ang)
        sin_ref[...] = jnp.sin(ang)
        plain_step()

    @pl.when(j == 1)
    def _():
        @pl.when(i + 1 < n_row_blocks)
        def _():
            x_copy(i + 1).start()

    @pl.when(jnp.logical_and(j > 0, j < n_plain))
    def _():
        plain_step()

    @pl.when(j >= n_plain)
    def _():
        cos = cos_ref[...]
        sin = sin_ref[...]
        scale = jnp.where(j - n_plain >= n_q_blocks, k_scale, 1.0).astype(_F32)
        for t in range(heads_per_block):
            z = head_dot(t)
            t1 = z[:, :half]
            t2 = z[:, half:]
            lo = t * head_dim
            zr_ref[:, lo:lo + half] = ((t1 * cos - t2 * sin) * scale).astype(zr_ref.dtype)
            zr_ref[:, lo + half:lo + head_dim] = ((t1 * sin + t2 * cos) * scale).astype(zr_ref.dtype)


def _proj_in(x, gain, pos, freqs, w, *, ret_dim, head_dim, tm=1024, tn=1024):
    s, d = x.shape
    n = w.shape[1]
    assert s % tm == 0 and n % tn == 0 and ret_dim % tn == 0 and tn % head_dim == 0
    half = head_dim // 2
    n_rot = 2 * ret_dim // tn
    n_plain = n // tn - n_rot
    assert n_plain >= 2
    kern = functools.partial(
        _proj_in_kernel, n_row_blocks=s // tm, n_plain=n_plain, n_q_blocks=ret_dim // tn,
        heads_per_block=tn // head_dim, head_dim=head_dim, k_scale=head_dim ** -0.5, row_chunk=128)
    return pl.pallas_call(
        kern,
        grid=(s // tm, n // tn),
        in_specs=[
            pl.BlockSpec(memory_space=pl.ANY),
            pl.BlockSpec((1, d), lambda i, j: (0, 0)),
            pl.BlockSpec((tm, 1), lambda i, j: (i, 0)),
            pl.BlockSpec((1, half), lambda i, j: (0, 0)),
            pl.BlockSpec((d, tn), lambda i, j: (0, jnp.where(j < n_plain, j + n_rot, j - n_plain))),
        ],
        out_specs=[
            pl.BlockSpec((tm, tn), lambda i, j: (i, jnp.maximum(j - n_plain, 0))),
            pl.BlockSpec((tm, tn), lambda i, j: (i, jnp.minimum(j, n_plain - 1))),
        ],
        out_shape=[jax.ShapeDtypeStruct((s, n_rot * tn), _BF16),
                   jax.ShapeDtypeStruct((s, n_plain * tn), _F32)],
        scratch_shapes=[pltpu.VMEM((tm, d), _F32), pltpu.VMEM((tm, d), _BF16),
                        pltpu.VMEM((tm, half), _F32), pltpu.VMEM((tm, half), _F32),
                        pltpu.SemaphoreType.DMA],
        compiler_params=_params("arbitrary", "arbitrary"),
        name="proj_in",
    )(x, gain, pos, freqs, w)


def _contract_rows(a, b):
    return lax.dot_general(a, b, (((0,), (0,)), ((), ())), preferred_element_type=_F32)


def _ret_state_kernel(lgb_ref, k_ref, v_ref, b_ref, state_ref, vdec_ref, cdec_ref, *, chunk, heads, hd):
    t = pl.program_id(0)

    @pl.when(t == 0)
    def _():
        state_ref[...] = jnp.zeros_like(state_ref)
        idx = lax.broadcasted_iota(jnp.int32, (chunk, hd), 0).astype(_F32)
        for h in range(heads):
            lgb = jax.nn.log_sigmoid(lgb_ref[h])
            vdec_ref[h] = jnp.exp(idx * lgb)
            cdec_ref[h] = jnp.broadcast_to(jnp.exp(chunk * lgb), (1, hd))

    for h in range(heads):
        cols = slice(h * hd, (h + 1) * hd)
        b_ref[h, 0] = state_ref[h].astype(_BF16)
        v_dec = (v_ref[:, cols] * vdec_ref[h]).astype(_BF16)
        state_ref[h] = state_ref[h] * cdec_ref[h] + _contract_rows(k_ref[:, cols], v_dec)


def _ret_out_kernel(lgf_ref, lgb_ref, q_ref, k_ref, v_ref, g_ref, b_ref, hn_ref, o_ref,
                    state_ref, dm_ref, dec_ref, cdec_ref, *, chunk, heads, hd):
    c = pl.program_id(0)

    @pl.when(c == 0)
    def _():
        state_ref[...] = jnp.zeros_like(state_ref)
        rel = (lax.broadcasted_iota(jnp.int32, (chunk, chunk), 0)
               - lax.broadcasted_iota(jnp.int32, (chunk, chunk), 1)).astype(_F32)
        idx = lax.broadcasted_iota(jnp.int32, (chunk, hd), 0).astype(_F32)
        for h in range(heads):
            lgf = jax.nn.log_sigmoid(lgf_ref[h])
            lgb = jax.nn.log_sigmoid(lgb_ref[h])
            dm_ref[h] = jnp.exp(jnp.where(rel >= 0, rel * lgf, -rel * lgb))
            dec_ref[0, h] = jnp.exp((idx + 1.0) * lgf)
            dec_ref[1, h] = jnp.exp((chunk - idx) * lgb)
            dec_ref[2, h] = jnp.exp((chunk - 1.0 - idx) * lgf)
            cdec_ref[h] = jnp.broadcast_to(jnp.exp(chunk * lgf), (1, hd))

    for h in range(heads):
        cols = slice(h * hd, (h + 1) * hd)
        q = q_ref[:, cols]
        k = k_ref[:, cols]
        v = v_ref[:, cols]
        scores = lax.dot_general(q, k, (((1,), (1,)), ((), ())), preferred_element_type=_F32)
        scores = scores * dm_ref[h]
        out = jnp.dot(scores.astype(_BF16), v.astype(_BF16), preferred_element_type=_F32)
        out += jnp.dot(q, state_ref[h].astype(_BF16), preferred_element_type=_F32) * dec_ref[0, h]
        out += jnp.dot(q, b_ref[h, 0], preferred_element_type=_F32) * dec_ref[1, h]

        out = out * lax.rsqrt(jnp.mean(out * out, axis=-1, keepdims=True) + EPS)
        out = out * hn_ref[:, cols]
        o_ref[:, cols] = (out * _silu(g_ref[:, cols])).astype(o_ref.dtype)

        v_dec = (v * dec_ref[2, h]).astype(_BF16)
        state_ref[h] = state_ref[h] * cdec_ref[h] + _contract_rows(k, v_dec)


def _retention(zr, zf, logit_f, logit_b, head_gain, *, heads, head_dim, chunk=256):
    s = zr.shape[0]
    assert s % chunk == 0
    nc = s // chunk
    hd = head_dim
    rd = heads * hd
    decay_spec = pl.BlockSpec((heads, 1, 1), lambda c: (0, 0, 0))

    b_states = pl.pallas_call(
        functools.partial(_ret_state_kernel, chunk=chunk, heads=heads, hd=hd),
        grid=(nc,),
        in_specs=[
            decay_spec,
            pl.BlockSpec((chunk, rd), lambda t: (nc - 1 - t, 1)),
            pl.BlockSpec((chunk, rd), lambda t: (nc - 1 - t, 0)),
        ],
        out_specs=pl.BlockSpec((heads, 1, hd, hd), lambda t: (0, nc - 1 - t, 0, 0)),
        out_shape=jax.ShapeDtypeStruct((heads, nc, hd, hd), _BF16),
        scratch_shapes=[pltpu.VMEM((heads, hd, hd), _F32), pltpu.VMEM((heads, chunk, hd), _F32),
                        pltpu.VMEM((heads, 1, hd), _F32)],
        compiler_params=_params("arbitrary"),
        name="ret_state",
    )(logit_b, zr, zf)

    return pl.pallas_call(
        functools.partial(_ret_out_kernel, chunk=chunk, heads=heads, hd=hd),
        grid=(nc,),
        in_specs=[
            decay_spec,
            decay_spec,
            pl.BlockSpec((chunk, rd), lambda c: (c, 0)),
            pl.BlockSpec((chunk, rd), lambda c: (c, 1)),
            pl.BlockSpec((chunk, rd), lambda c: (c, 0)),
            pl.BlockSpec((chunk, rd), lambda c: (c, 1)),
            pl.BlockSpec((heads, 1, hd, hd), lambda c: (0, c, 0, 0)),
            pl.BlockSpec((1, rd), lambda c: (0, 0)),
        ],
        out_specs=pl.BlockSpec((chunk, rd), lambda c: (c, 0)),
        out_shape=jax.ShapeDtypeStruct((s, rd), _BF16),
        scratch_shapes=[pltpu.VMEM((heads, hd, hd), _F32), pltpu.VMEM((heads, chunk, chunk), _F32),
                        pltpu.VMEM((3, heads, chunk, hd), _F32), pltpu.VMEM((heads, 1, hd), _F32)],
        compiler_params=_params("arbitrary"),
        name="ret_out",
    )(logit_f, logit_b, zr, zr, zf, zf, b_states, head_gain)


def _window_sum(p, window):
    rows = p.shape[0]
    assert window >= 2 and window & (window - 1) == 0
    shifted = lambda a, k: pltpu.roll(a, k % rows, axis=0)
    s = shifted(p, 1) + p
    width = 2
    while width < window:
        s = shifted(s, width // 2) + shifted(s, -(width // 2))
        width *= 2
    return s


def _pool_kernel(prev_ref, cur_ref, next_ref, w_ref, scale_ref, o_ref, pad_ref, *, tm, seq, group_dim):
    i = pl.program_id(0)
    n_blocks = pl.num_programs(0)
    halo = POOL_HALO
    pad_ref[0:halo, :] = jnp.where(i > 0, prev_ref[...], 0.0)
    pad_ref[halo:halo + tm, :] = cur_ref[...]
    pad_ref[halo + tm:, :] = jnp.where(i < n_blocks - 1, next_ref[...], 0.0)

    row = i * tm + lax.broadcasted_iota(jnp.int32, (tm, 1), 0)
    for gi, window in enumerate(POOL_WINDOWS):
        lo = window // 2
        hi = window - 1 - lo
        cols = slice(gi * group_dim, (gi + 1) * group_dim)
        total = _window_sum(pad_ref[:, cols], window)[halo:halo + tm]
        count = (jnp.minimum(row + hi + 1, seq) - jnp.maximum(row - lo, 0)).astype(_F32)
        pooled = total / count - cur_ref[:, cols]
        mapped = jnp.dot(pooled.astype(_BF16), w_ref[gi], preferred_element_type=_F32)
        o_ref[:, cols] = (mapped * scale_ref[:, cols]).astype(o_ref.dtype)


def _pool(z, pool_w, pool_scale, *, tm=512):
    s, n = z.shape
    groups, group_dim, _ = pool_w.shape
    pool_dim = groups * group_dim
    assert groups == len(POOL_WINDOWS) and max(POOL_WINDOWS) // 2 <= POOL_HALO
    assert s % tm == 0 and tm % POOL_HALO == 0 and n % pool_dim == 0
    col = n // pool_dim - 1
    r = tm // POOL_HALO
    last = s // POOL_HALO - 1
    return pl.pallas_call(
        functools.partial(_pool_kernel, tm=tm, seq=s, group_dim=group_dim),
        grid=(s // tm,),
        in_specs=[
            pl.BlockSpec((POOL_HALO, pool_dim), lambda i: (jnp.maximum(i * r - 1, 0), col)),
            pl.BlockSpec((tm, pool_dim), lambda i: (i, col)),
            pl.BlockSpec((POOL_HALO, pool_dim), lambda i: (jnp.minimum((i + 1) * r, last), col)),
            pl.BlockSpec((groups, group_dim, group_dim), lambda i: (0, 0, 0)),
            pl.BlockSpec((1, pool_dim), lambda i: (0, 0)),
        ],
        out_specs=pl.BlockSpec((tm, pool_dim), lambda i: (i, 0)),
        out_shape=jax.ShapeDtypeStruct((s, pool_dim), _BF16),
        scratch_shapes=[pltpu.VMEM((tm + 2 * POOL_HALO, pool_dim), _F32)],
        compiler_params=_params("parallel"),
        name="pool",
    )(z, z, z, pool_w, pool_scale)


def _proj_out_kernel(x_ref, ret_ref, pool_ref, wr_ref, wp_ref, o_ref):
    acc = jnp.dot(ret_ref[...], wr_ref[...], preferred_element_type=_F32)
    acc += jnp.dot(pool_ref[...], wp_ref[...], preferred_element_type=_F32)
    o_ref[...] = x_ref[...] + acc


def _proj_out(x, ret, pool, w, *, tm=1024, tn=1024):
    s, d = x.shape
    kr = ret.shape[1]
    kp = pool.shape[1]
    assert kr == kp and w.shape == (kr + kp, d) and s % tm == 0 and d % tn == 0
    return pl.pallas_call(
        _proj_out_kernel,
        grid=(s // tm, d // tn),
        in_specs=[
            pl.BlockSpec((tm, tn), lambda i, j: (i, j)),
            pl.BlockSpec((tm, kr), lambda i, j: (i, 0)),
            pl.BlockSpec((tm, kp), lambda i, j: (i, 0)),
            pl.BlockSpec((kr, tn), lambda i, j: (0, j)),
            pl.BlockSpec((kp, tn), lambda i, j: (1, j)),
        ],
        out_specs=pl.BlockSpec((tm, tn), lambda i, j: (i, j)),
        out_shape=jax.ShapeDtypeStruct((s, d), _F32),
        compiler_params=_params("parallel", "arbitrary"),
        name="proj_out",
    )(x, ret, pool, w, w)


def kernel(x, positions, ffn1_norm, ffn1_w_gate, ffn1_w_up, ffn1_w_down, mix_norm, w_in,
           ret_decay_fwd, ret_decay_bwd, ret_head_norm, pool_w, pool_scale, w_out,
           ffn2_norm, ffn2_w_gate, ffn2_w_up, ffn2_w_down, final_norm):
    b, s, d = x.shape
    assert b == 1, "retention scan is written for a single sequence"
    depth = ffn1_norm.shape[0]
    heads = ret_decay_fwd.shape[1]
    ret_dim = ret_head_norm.shape[1]
    head_dim = ret_dim // heads

    freqs = 1.0 / (ROPE_BASE ** (jnp.arange(0, head_dim, 2, dtype=_F32) / head_dim))
    freqs = freqs.reshape(1, head_dim // 2)
    pos = positions.reshape(s, 1)
    final_gain = final_norm.reshape(1, d)
    row = lambda v: v.reshape(1, -1).astype(_F32)
    bf = lambda w: w.astype(_BF16)

    xs = x.reshape(s, d)
    for layer in range(depth):
        last = layer == depth - 1
        xs, w2_gate, w2_up, w2_down, w_in_bf, w_out_bf = _ffn(
            xs, row(ffn1_norm[layer]), ffn1_w_gate[layer], ffn1_w_up[layer], ffn1_w_down[layer],
            final_gain, final_norm=False,
            casts=(ffn2_w_gate[layer], ffn2_w_up[layer], ffn2_w_down[layer], w_in[layer], w_out[layer]))
        zr, zf = _proj_in(xs, row(mix_norm[layer]), pos, freqs, w_in_bf,
                          ret_dim=ret_dim, head_dim=head_dim)
        ret = _retention(zr, zf, ret_decay_fwd[layer].reshape(heads, 1, 1).astype(_F32),
                         ret_decay_bwd[layer].reshape(heads, 1, 1).astype(_F32),
                         row(ret_head_norm[layer]), heads=heads, head_dim=head_dim)
        pool = _pool(zf, bf(pool_w[layer]), row(pool_scale[layer]))
        xs = _proj_out(xs, ret, pool, w_out_bf)
        xs, = _ffn(xs, row(ffn2_norm[layer]), w2_gate, w2_up, w2_down, final_gain, final_norm=last)
    return xs.reshape(b, s, d)
```

```python
import functools

import jax
import jax.numpy as jnp
from jax import lax
from jax.experimental import pallas as pl
from jax.experimental.pallas import tpu as pltpu

EPS = 1e-6
ROPE_BASE = 10000.0
POOL_WINDOWS = (2, 4, 8, 16)
POOL_HALO = 16

_F32 = jnp.float32
_BF16 = jnp.bfloat16
_VMEM_LIMIT_BYTES = 62 * 1024 * 1024


def _params(*semantics):
    return pltpu.CompilerParams(dimension_semantics=semantics, vmem_limit_bytes=_VMEM_LIMIT_BYTES)


def _rmsnorm(x, gain):
    return x * lax.rsqrt(jnp.mean(x * x, axis=-1, keepdims=True) + EPS) * gain


def _silu(g):
    return g * jax.nn.sigmoid(g)


def _ffn_head_kernel(x_hbm, gain_ref, wg_ref, wu_ref, wd_ref, o_hbm, wg_hbm, wu_hbm, wd_hbm,
                     acc_ref, h_ref, wg_bf, wu_bf, wd_bf, sem, w_sem,
                     *, n_steps, tf, row_chunk, row_part, col_chunk):
    j = pl.program_id(0)
    tm, d = acc_ref.shape

    def weight_copies(step):
        cols = pl.ds(pl.multiple_of(step * tf, tf), tf)
        return (pltpu.make_async_copy(wg_bf, wg_hbm.at[:, cols], w_sem.at[0]),
                pltpu.make_async_copy(wu_bf, wu_hbm.at[:, cols], w_sem.at[1]),
                pltpu.make_async_copy(wd_bf, wd_hbm.at[cols, :], w_sem.at[2]))

    @pl.when(j == 0)
    def _():
        copy = pltpu.make_async_copy(x_hbm.at[pl.ds(0, tm), :], acc_ref, sem)
        copy.start()
        copy.wait()
        for r in range(0, tm, row_chunk):
            rows = slice(r, r + row_chunk)
            h_ref[rows, :] = _rmsnorm(acc_ref[rows, :], gain_ref[...]).astype(_BF16)

    @pl.when(j > 0)
    def _():
        for copy in weight_copies(j - 1):
            copy.wait()

    wg_bf[...] = wg_ref[...].astype(_BF16)
    wu_bf[...] = wu_ref[...].astype(_BF16)
    wd_bf[...] = wd_ref[...].astype(_BF16)
    for copy in weight_copies(j):
        copy.start()

    parts = [slice(r, r + row_part) for r in range(0, tm, row_part)]
    acts = []
    for rows in parts:
        h = h_ref[rows, :]
        g = jnp.dot(h, wg_bf[...], preferred_element_type=_F32)
        u = jnp.dot(h, wu_bf[...], preferred_element_type=_F32)
        acts.append((_silu(g) * u * 0.5).astype(_BF16))
    for rows, a in zip(parts, acts):
        for c in range(0, d, col_chunk):
            cols = slice(c, c + col_chunk)
            acc_ref[rows, cols] += jnp.dot(a, wd_bf[:, cols], preferred_element_type=_F32)

    @pl.when(j == n_steps - 1)
    def _():
        for copy in weight_copies(j):
            copy.wait()
        copy = pltpu.make_async_copy(acc_ref, o_hbm, sem)
        copy.start()
        copy.wait()


def _ffn_head(x, gain, wg, wu, wd, *, rows, tf):
    s, d = x.shape
    f = wg.shape[1]
    assert f % tf == 0 and rows % 256 == 0
    n_steps = f // tf
    any_spec = pl.BlockSpec(memory_space=pl.ANY)
    return pl.pallas_call(
        functools.partial(_ffn_head_kernel, n_steps=n_steps, tf=tf, row_chunk=128, row_part=256,
                          col_chunk=1024),
        grid=(n_steps,),
        in_specs=[
            any_spec,
            pl.BlockSpec((1, d), lambda j: (0, 0)),
            pl.BlockSpec((d, tf), lambda j: (0, j)),
            pl.BlockSpec((d, tf), lambda j: (0, j)),
            pl.BlockSpec((tf, d), lambda j: (j, 0)),
        ],
        out_specs=[any_spec, any_spec, any_spec, any_spec],
        out_shape=[jax.ShapeDtypeStruct((rows, d), _F32), jax.ShapeDtypeStruct(wg.shape, _BF16),
                   jax.ShapeDtypeStruct(wu.shape, _BF16), jax.ShapeDtypeStruct(wd.shape, _BF16)],
        scratch_shapes=[pltpu.VMEM((rows, d), _F32), pltpu.VMEM((rows, d), _BF16),
                        pltpu.VMEM((d, tf), _BF16), pltpu.VMEM((d, tf), _BF16), pltpu.VMEM((tf, d), _BF16),
                        pltpu.SemaphoreType.DMA, pltpu.SemaphoreType.DMA((3,))],
        compiler_params=_params("arbitrary"),
        name="ffn_head",
    )(x, gain, wg, wu, wd)


def _ffn_kernel(x_hbm, gain_ref, wg_ref, wu_ref, wd_ref, fgain_ref, *rest, head_blocks,
                n_casts, n_steps, n_row_blocks, tf, tail_subs, final_norm, row_chunk, col_chunk):
    if head_blocks:
        head_hbm, rest = rest[0], rest[1:]
    cast_in = rest[:n_casts]
    o_ref = rest[n_casts]
    cast_out = rest[n_casts + 1:2 * n_casts + 1]
    x_buf, h_ref, x_sem = rest[2 * n_casts + 1:]
    i = pl.program_id(0)
    j = pl.program_id(1)
    tm, d = o_ref.shape
    computed = i >= head_blocks

    def x_copy(block):
        return pltpu.make_async_copy(x_hbm.at[pl.ds(block * tm, tm), :], x_buf, x_sem)

    def activation(k):
        cols = slice(k * tf, (k + 1) * tf)
        h = h_ref[...]
        g = jnp.dot(h, wg_ref[:, cols], preferred_element_type=_F32)
        u = jnp.dot(h, wu_ref[:, cols], preferred_element_type=_F32)
        return (_silu(g) * u * 0.5).astype(_BF16)

    def cast_blocks():
        for src, dst in zip(cast_in, cast_out):
            dst[...] = src[...].astype(dst.dtype)

    def step(n_subs):
        acts = [activation(k) for k in range(n_subs)]
        cast_blocks()
        for k, a in enumerate(acts):
            for c in range(0, d, col_chunk):
                cols = slice(c, c + col_chunk)
                o_ref[:, cols] += jnp.dot(a, wd_ref[k * tf:(k + 1) * tf, cols],
                                          preferred_element_type=_F32)

    if head_blocks:
        @pl.when(i < head_blocks)
        def _():
            @pl.when(j == 0)
            def _():
                copy = pltpu.make_async_copy(head_hbm.at[pl.ds(i * tm, tm), :], x_buf, x_sem)
                copy.start()
                copy.wait()
                for r in range(0, tm, row_chunk):
                    rows = slice(r, r + row_chunk)
                    o_ref[rows, :] = x_buf[rows, :]

            cast_blocks()

    @pl.when(jnp.logical_and(j == 0, computed))
    def _():
        if not head_blocks:
            @pl.when(i == 0)
            def _():
                x_copy(0).start()

        x_copy(i).wait()
        for r in range(0, tm, row_chunk):
            rows = slice(r, r + row_chunk)
            xr = x_buf[rows, :]
            h_ref[rows, :] = _rmsnorm(xr, gain_ref[...]).astype(_BF16)
            o_ref[rows, :] = xr
        step(2)

    @pl.when(j == 1)
    def _():
        @pl.when(jnp.logical_and(i + 1 < n_row_blocks, i + 1 >= head_blocks))
        def _():
            x_copy(i + 1).start()

    @pl.when(jnp.logical_and(jnp.logical_and(j > 0, j < n_steps - 1), computed))
    def _():
        step(2)

    @pl.when(jnp.logical_and(j == n_steps - 1, computed))
    def _():
        step(tail_subs)
        if final_norm:
            for r in range(0, tm, row_chunk):
                rows = slice(r, r + row_chunk)
                o_ref[rows, :] = _rmsnorm(o_ref[rows, :], fgain_ref[...])


_CAST_BLOCK_ELEMS = 128 * 1024


def _cast_specs(w, n_steps, total_steps):
    rows, width = w.shape
    shapes = [(r, k) for r in (32, 16) for k in (1, 2, 4)
              if rows % r == 0 and width % (k * 128) == 0 and r * (width // k) <= _CAST_BLOCK_ELEMS
              and (rows // r) * k <= total_steps]
    if not shapes:
        return None
    r, k = max(shapes, key=lambda rk: (rk[0] * (width // rk[1]), rk[0]))
    n_blocks = (rows // r) * k

    def index(i, j):
        b = jnp.minimum(i * n_steps + j, n_blocks - 1)
        return b // k, b % k

    block = (r, width // k)
    return pl.BlockSpec(block, index), pl.BlockSpec(block, index), jax.ShapeDtypeStruct(w.shape, _BF16)


def _ffn(x, gain, wg, wu, wd, fgain, *, final_norm, casts=(), tm=512, tf=256):
    s, d = x.shape
    f = wg.shape[1]
    assert s % tm == 0 and f % tf == 0
    n_steps = pl.cdiv(f, 2 * tf)
    assert n_steps >= 3
    tail_subs = (f - (n_steps - 1) * 2 * tf) // tf
    n_row_blocks = s // tm
    head_blocks = min(2, n_row_blocks - 1) if wg.dtype != _BF16 else 0
    head = ()
    if head_blocks:
        head_rows, wg, wu, wd = _ffn_head(x, gain, wg, wu, wd, rows=head_blocks * tm, tf=tf)
        head = (head_rows,)
    w_col = lambda i, j: jnp.where(i < head_blocks, 0, j)
    all_specs = [_cast_specs(w, n_steps, n_row_blocks * n_steps) for w in casts]
    hosted = [w for w, spec in zip(casts, all_specs) if spec is not None]
    cast_specs = [spec for spec in all_specs if spec is not None]
    kern = functools.partial(
        _ffn_kernel, head_blocks=head_blocks, n_casts=len(hosted), n_steps=n_steps, n_row_blocks=n_row_blocks,
        tf=tf, tail_subs=tail_subs, final_norm=final_norm, row_chunk=128, col_chunk=1024)
    out, *hosted_bf = pl.pallas_call(
        kern,
        grid=(n_row_blocks, n_steps),
        in_specs=[
            pl.BlockSpec(memory_space=pl.ANY),
            pl.BlockSpec((1, d), lambda i, j: (0, 0)),
            pl.BlockSpec((d, 2 * tf), lambda i, j: (0, w_col(i, j))),
            pl.BlockSpec((d, 2 * tf), lambda i, j: (0, w_col(i, j))),
            pl.BlockSpec((2 * tf, d), lambda i, j: (w_col(i, j), 0)),
            pl.BlockSpec((1, d), lambda i, j: (0, 0)),
        ] + [pl.BlockSpec(memory_space=pl.ANY)] * len(head) + [spec[0] for spec in cast_specs],
        out_specs=[pl.BlockSpec((tm, d), lambda i, j: (i, 0))] + [spec[1] for spec in cast_specs],
        out_shape=[jax.ShapeDtypeStruct((s, d), _F32)] + [spec[2] for spec in cast_specs],
        scratch_shapes=[pltpu.VMEM((tm, d), _F32), pltpu.VMEM((tm, d), _BF16),
                        pltpu.SemaphoreType.DMA],
        compiler_params=_params("arbitrary", "arbitrary"),
        name="ffn",
    )(x, gain, wg.astype(_BF16), wu.astype(_BF16), wd.astype(_BF16), fgain, *head, *hosted)
    hosted_bf = iter(hosted_bf)
    return [out] + [next(hosted_bf) if spec is not None else w.astype(_BF16)
                    for w, spec in zip(casts, all_specs)]


def _proj_in_kernel(x_hbm, gain_ref, pos_ref, freq_ref, w_ref, zr_ref, zf_ref,
                    x_buf, h_ref, cos_ref, sin_ref, x_sem,
                    *, n_row_blocks, n_plain, n_q_blocks, heads_per_block, head_dim, k_scale, row_chunk):
    i = pl.program_id(0)
    j = pl.program_id(1)
    tm = x_buf.shape[0]
    half = head_dim // 2

    def x_copy(block):
        return pltpu.make_async_copy(x_hbm.at[pl.ds(block * tm, tm), :], x_buf, x_sem)

    def head_dot(t):
        cols = slice(t * head_dim, (t + 1) * head_dim)
        return jnp.dot(h_ref[...], w_ref[:, cols], preferred_element_type=_F32)

    def plain_step():
        for t in range(heads_per_block):
            zf_ref[:, t * head_dim:(t + 1) * head_dim] = head_dot(t)

    @pl.when(j == 0)
    def _():
        @pl.when(i == 0)
        def _():
            x_copy(0).start()

        x_copy(i).wait()
        for r in range(0, tm, row_chunk):
            rows = slice(r, r + row_chunk)
            h_ref[rows, :] = _rmsnorm(x_buf[rows, :], gain_ref[...]).astype(_BF16)
        ang = pos_ref[...].astype(_F32) * freq_ref[...]
        cos_ref[...] = jnp.cos(ang)
        sin_ref[...] = jnp.sin(ang)
        plain_step()

    @pl.when(j == 1)
    def _():
        @pl.when(i + 1 < n_row_blocks)
        def _():
            x_copy(i + 1).start()

    @pl.when(jnp.logical_and(j > 0, j < n_plain))
    def _():
        plain_step()

    @pl.when(j >= n_plain)
    def _():
        cos = cos_ref[...]
        sin = sin_ref[...]
        scale = jnp.where(j - n_plain >= n_q_blocks, k_scale, 1.0).astype(_F32)
        for t in range(heads_per_block):
            z = head_dot(t)
            t1 = z[:, :half]
            t2 = z[:, half:]
            lo = t * head_dim
            zr_ref[:, lo:lo + half] = ((t1 * cos - t2 * sin) * scale).astype(zr_ref.dtype)
            zr_ref[:, lo + half:lo + head_dim] = ((t1 * sin + t2 * cos) * scale).astype(zr_ref.dtype)


def _proj_in(x, gain, pos, freqs, w, *, ret_dim, head_dim, tm=1024, tn=1024):
    s, d = x.shape
    n = w.shape[1]
    assert s % tm == 0 and n % tn == 0 and ret_dim % tn == 0 and tn % head_dim == 0
    half = head_dim // 2
    n_rot = 2 * ret_dim // tn
    n_plain = n // tn - n_rot
    assert n_plain >= 2
    kern = functools.partial(
        _proj_in_kernel, n_row_blocks=s // tm, n_plain=n_plain, n_q_blocks=ret_dim // tn,
        heads_per_block=tn // head_dim, head_dim=head_dim, k_scale=head_dim ** -0.5, row_chunk=128)
    return pl.pallas_call(
        kern,
        grid=(s // tm, n // tn),
        in_specs=[
            pl.BlockSpec(memory_space=pl.ANY),
            pl.BlockSpec((1, d), lambda i, j: (0, 0)),
            pl.BlockSpec((tm, 1), lambda i, j: (i, 0)),
            pl.BlockSpec((1, half), lambda i, j: (0, 0)),
            pl.BlockSpec((d, tn), lambda i, j: (0, jnp.where(j < n_plain, j + n_rot, j - n_plain))),
        ],
        out_specs=[
            pl.BlockSpec((tm, tn), lambda i, j: (i, jnp.maximum(j - n_plain, 0))),
            pl.BlockSpec((tm, tn), lambda i, j: (i, jnp.minimum(j, n_plain - 1))),
        ],
        out_shape=[jax.ShapeDtypeStruct((s, n_rot * tn), _BF16),
                   jax.ShapeDtypeStruct((s, n_plain * tn), _F32)],
        scratch_shapes=[pltpu.VMEM((tm, d), _F32), pltpu.VMEM((tm, d), _BF16),
                        pltpu.VMEM((tm, half), _F32), pltpu.VMEM((tm, half), _F32),
                        pltpu.SemaphoreType.DMA],
        compiler_params=_params("arbitrary", "arbitrary"),
        name="proj_in",
    )(x, gain, pos, freqs, w)


def _contract_rows(a, b):
    return lax.dot_general(a, b, (((0,), (0,)), ((), ())), preferred_element_type=_F32)


def _ret_state_kernel(lgb_ref, k_ref, v_ref, b_ref, state_ref, vdec_ref, cdec_ref, *, chunk, heads, hd):
    t = pl.program_id(0)

    @pl.when(t == 0)
    def _():
        state_ref[...] = jnp.zeros_like(state_ref)
        idx = lax.broadcasted_iota(jnp.int32, (chunk, hd), 0).astype(_F32)
        for h in range(heads):
            lgb = jax.nn.log_sigmoid(lgb_ref[h])
            vdec_ref[h] = jnp.exp(idx * lgb)
            cdec_ref[h] = jnp.broadcast_to(jnp.exp(chunk * lgb), (1, hd))

    for h in range(heads):
        cols = slice(h * hd, (h + 1) * hd)
        b_ref[h, 0] = state_ref[h].astype(_BF16)
        v_dec = (v_ref[:, cols] * vdec_ref[h]).astype(_BF16)
        state_ref[h] = state_ref[h] * cdec_ref[h] + _contract_rows(k_ref[:, cols], v_dec)


def _ret_out_kernel(lgf_ref, lgb_ref, q_ref, k_ref, v_ref, g_ref, b_ref, hn_ref, o_ref,
                    state_ref, dm_ref, dec_ref, cdec_ref, *, chunk, heads, hd):
    c = pl.program_id(0)

    @pl.when(c == 0)
    def _():
        state_ref[...] = jnp.zeros_like(state_ref)
        rel = (lax.broadcasted_iota(jnp.int32, (chunk, chunk), 0)
               - lax.broadcasted_iota(jnp.int32, (chunk, chunk), 1)).astype(_F32)
        idx = lax.broadcasted_iota(jnp.int32, (chunk, hd), 0).astype(_F32)
        for h in range(heads):
            lgf = jax.nn.log_sigmoid(lgf_ref[h])
            lgb = jax.nn.log_sigmoid(lgb_ref[h])
            dm_ref[h] = jnp.exp(jnp.where(rel >= 0, rel * lgf, -rel * lgb))
            dec_ref[0, h] = jnp.exp((idx + 1.0) * lgf)
            dec_ref[1, h] = jnp.exp((chunk - idx) * lgb)
            dec_ref[2, h] = jnp.exp((chunk - 1.0 - idx) * lgf)
            cdec_ref[h] = jnp.broadcast_to(jnp.exp(chunk * lgf), (1, hd))

    for h in range(heads):
        cols = slice(h * hd, (h + 1) * hd)
        q = q_ref[:, cols]
        k = k_ref[:, cols]
        v = v_ref[:, cols]
        scores = lax.dot_general(q, k, (((1,), (1,)), ((), ())), preferred_element_type=_F32)
        scores = scores * dm_ref[h]
        out = jnp.dot(scores.astype(_BF16), v.astype(_BF16), preferred_element_type=_F32)
        out += jnp.dot(q, state_ref[h].astype(_BF16), preferred_element_type=_F32) * dec_ref[0, h]
        out += jnp.dot(q, b_ref[h, 0], preferred_element_type=_F32) * dec_ref[1, h]

        out = out * lax.rsqrt(jnp.mean(out * out, axis=-1, keepdims=True) + EPS)
        out = out * hn_ref[:, cols]
        o_ref[:, cols] = (out * _silu(g_ref[:, cols])).astype(o_ref.dtype)

        v_dec = (v * dec_ref[2, h]).astype(_BF16)
        state_ref[h] = state_ref[h] * cdec_ref[h] + _contract_rows(k, v_dec)


def _retention(zr, zf, logit_f, logit_b, head_gain, *, heads, head_dim, chunk=256):
    s = zr.shape[0]
    assert s % chunk == 0
    nc = s // chunk
    hd = head_dim
    rd = heads * hd
    decay_spec = pl.BlockSpec((heads, 1, 1), lambda c: (0, 0, 0))

    b_states = pl.pallas_call(
        functools.partial(_ret_state_kernel, chunk=chunk, heads=heads, hd=hd),
        grid=(nc,),
        in_specs=[
            decay_spec,
            pl.BlockSpec((chunk, rd), lambda t: (nc - 1 - t, 1)),
            pl.BlockSpec((chunk, rd), lambda t: (nc - 1 - t, 0)),
        ],
        out_specs=pl.BlockSpec((heads, 1, hd, hd), lambda t: (0, nc - 1 - t, 0, 0)),
        out_shape=jax.ShapeDtypeStruct((heads, nc, hd, hd), _BF16),
        scratch_shapes=[pltpu.VMEM((heads, hd, hd), _F32), pltpu.VMEM((heads, chunk, hd), _F32),
                        pltpu.VMEM((heads, 1, hd), _F32)],
        compiler_params=_params("arbitrary"),
        name="ret_state",
    )(logit_b, zr, zf)

    return pl.pallas_call(
        functools.partial(_ret_out_kernel, chunk=chunk, heads=heads, hd=hd),
        grid=(nc,),
        in_specs=[
            decay_spec,
            decay_spec,
            pl.BlockSpec((chunk, rd), lambda c: (c, 0)),
            pl.BlockSpec((chunk, rd), lambda c: (c, 1)),
            pl.BlockSpec((chunk, rd), lambda c: (c, 0)),
            pl.BlockSpec((chunk, rd), lambda c: (c, 1)),
            pl.BlockSpec((heads, 1, hd, hd), lambda c: (0, c, 0, 0)),
            pl.BlockSpec((1, rd), lambda c: (0, 0)),
        ],
        out_specs=pl.BlockSpec((chunk, rd), lambda c: (c, 0)),
        out_shape=jax.ShapeDtypeStruct((s, rd), _BF16),
        scratch_shapes=[pltpu.VMEM((heads, hd, hd), _F32), pltpu.VMEM((heads, chunk, chunk), _F32),
                        pltpu.VMEM((3, heads, chunk, hd), _F32), pltpu.VMEM((heads, 1, hd), _F32)],
        compiler_params=_params("arbitrary"),
        name="ret_out",
    )(logit_f, logit_b, zr, zr, zf, zf, b_states, head_gain)


def _window_sum(p, window):
    rows = p.shape[0]
    assert window >= 2 and window & (window - 1) == 0
    shifted = lambda a, k: pltpu.roll(a, k % rows, axis=0)
    s = shifted(p, 1) + p
    width = 2
    while width < window:
        s = shifted(s, width // 2) + shifted(s, -(width // 2))
        width *= 2
    return s


def _pool_kernel(prev_ref, cur_ref, next_ref, w_ref, scale_ref, o_ref, pad_ref, *, tm, seq, group_dim):
    i = pl.program_id(0)
    n_blocks = pl.num_programs(0)
    halo = POOL_HALO
    pad_ref[0:halo, :] = jnp.where(i > 0, prev_ref[...], 0.0)
    pad_ref[halo:halo + tm, :] = cur_ref[...]
    pad_ref[halo + tm:, :] = jnp.where(i < n_blocks - 1, next_ref[...], 0.0)

    row = i * tm + lax.broadcasted_iota(jnp.int32, (tm, 1), 0)
    for gi, window in enumerate(POOL_WINDOWS):
        lo = window // 2
        hi = window - 1 - lo
        cols = slice(gi * group_dim, (gi + 1) * group_dim)
        total = _window_sum(pad_ref[:, cols], window)[halo:halo + tm]
        count = (jnp.minimum(row + hi + 1, seq) - jnp.maximum(row - lo, 0)).astype(_F32)
        pooled = total / count - cur_ref[:, cols]
        mapped = jnp.dot(pooled.astype(_BF16), w_ref[gi], preferred_element_type=_F32)
        o_ref[:, cols] = (mapped * scale_ref[:, cols]).astype(o_ref.dtype)


def _pool(z, pool_w, pool_scale, *, tm=512):
    s, n = z.shape
    groups, group_dim, _ = pool_w.shape
    pool_dim = groups * group_dim
    assert groups == len(POOL_WINDOWS) and max(POOL_WINDOWS) // 2 <= POOL_HALO
    assert s % tm == 0 and tm % POOL_HALO == 0 and n % pool_dim == 0
    col = n // pool_dim - 1
    r = tm // POOL_HALO
    last = s // POOL_HALO - 1
    return pl.pallas_call(
        functools.partial(_pool_kernel, tm=tm, seq=s, group_dim=group_dim),
        grid=(s // tm,),
        in_specs=[
            pl.BlockSpec((POOL_HALO, pool_dim), lambda i: (jnp.maximum(i * r - 1, 0), col)),
            pl.BlockSpec((tm, pool_dim), lambda i: (i, col)),
            pl.BlockSpec((POOL_HALO, pool_dim), lambda i: (jnp.minimum((i + 1) * r, last), col)),
            pl.BlockSpec((groups, group_dim, group_dim), lambda i: (0, 0, 0)),
            pl.BlockSpec((1, pool_dim), lambda i: (0, 0)),
        ],
        out_specs=pl.BlockSpec((tm, pool_dim), lambda i: (i, 0)),
        out_shape=jax.ShapeDtypeStruct((s, pool_dim), _BF16),
        scratch_shapes=[pltpu.VMEM((tm + 2 * POOL_HALO, pool_dim), _F32)],
        compiler_params=_params("parallel"),
        name="pool",
    )(z, z, z, pool_w, pool_scale)


def _proj_out_kernel(x_ref, ret_ref, pool_ref, wr_ref, wp_ref, o_ref):
    acc = jnp.dot(ret_ref[...], wr_ref[...], preferred_element_type=_F32)
    acc += jnp.dot(pool_ref[...], wp_ref[...], preferred_element_type=_F32)
    o_ref[...] = x_ref[...] + acc


def _proj_out(x, ret, pool, w, *, tm=1024, tn=1024):
    s, d = x.shape
    kr = ret.shape[1]
    kp = pool.shape[1]
    assert kr == kp and w.shape == (kr + kp, d) and s % tm == 0 and d % tn == 0
    return pl.pallas_call(
        _proj_out_kernel,
        grid=(s // tm, d // tn),
        in_specs=[
            pl.BlockSpec((tm, tn), lambda i, j: (i, j)),
            pl.BlockSpec((tm, kr), lambda i, j: (i, 0)),
            pl.BlockSpec((tm, kp), lambda i, j: (i, 0)),
            pl.BlockSpec((kr, tn), lambda i, j: (0, j)),
            pl.BlockSpec((kp, tn), lambda i, j: (1, j)),
        ],
        out_specs=pl.BlockSpec((tm, tn), lambda i, j: (i, j)),
        out_shape=jax.ShapeDtypeStruct((s, d), _F32),
        compiler_params=_params("parallel", "arbitrary"),
        name="proj_out",
    )(x, ret, pool, w, w)


def kernel(x, positions, ffn1_norm, ffn1_w_gate, ffn1_w_up, ffn1_w_down, mix_norm, w_in,
           ret_decay_fwd, ret_decay_bwd, ret_head_norm, pool_w, pool_scale, w_out,
           ffn2_norm, ffn2_w_gate, ffn2_w_up, ffn2_w_down, final_norm):
    b, s, d = x.shape
    assert b == 1, "retention scan is written for a single sequence"
    depth = ffn1_norm.shape[0]
    heads = ret_decay_fwd.shape[1]
    ret_dim = ret_head_norm.shape[1]
    head_dim = ret_dim // heads

    freqs = 1.0 / (ROPE_BASE ** (jnp.arange(0, head_dim, 2, dtype=_F32) / head_dim))
    freqs = freqs.reshape(1, head_dim // 2)
    pos = positions.reshape(s, 1)
    final_gain = final_norm.reshape(1, d)
    row = lambda v: v.reshape(1, -1).astype(_F32)
    bf = lambda w: w.astype(_BF16)

    xs = x.reshape(s, d)
    for layer in range(depth):
        last = layer == depth - 1
        xs, w2_gate, w2_up, w2_down, w_in_bf, w_out_bf = _ffn(
            xs, row(ffn1_norm[layer]), ffn1_w_gate[layer], ffn1_w_up[layer], ffn1_w_down[layer],
            final_gain, final_norm=False,
            casts=(ffn2_w_gate[layer], ffn2_w_up[layer], ffn2_w_down[layer], w_in[layer], w_out[layer]))
        zr, zf = _proj_in(xs, row(mix_norm[layer]), pos, freqs, w_in_bf,
                          ret_dim=ret_dim, head_dim=head_dim)
        ret = _retention(zr, zf, ret_decay_fwd[layer].reshape(heads, 1, 1).astype(_F32),
                         ret_decay_bwd[layer].reshape(heads, 1, 1).astype(_F32),
                         row(ret_head_norm[layer]), heads=heads, head_dim=head_dim)
        pool = _pool(zf, bf(pool_w[layer]), row(pool_scale[layer]))
        xs = _proj_out(xs, ret, pool, w_out_bf)
        xs, = _ffn(xs, row(ffn2_norm[layer]), w2_gate, w2_up, w2_down, final_gain, final_norm=last)
    return xs.reshape(b, s, d)
```

```python
import functools

import jax
import jax.numpy as jnp
from jax import lax
from jax.experimental import pallas as pl
from jax.experimental.pallas import tpu as pltpu

EPS = 1e-6
ROPE_BASE = 10000.0
POOL_WINDOWS = (2, 4, 8, 16)
POOL_HALO = 16

_F32 = jnp.float32
_BF16 = jnp.bfloat16
_VMEM_LIMIT_BYTES = 62 * 1024 * 1024


def _params(*semantics):
    return pltpu.CompilerParams(dimension_semantics=semantics, vmem_limit_bytes=_VMEM_LIMIT_BYTES)


def _rmsnorm(x, gain):
    return x * lax.rsqrt(jnp.mean(x * x, axis=-1, keepdims=True) + EPS) * gain


def _silu(g):
    return g * jax.nn.sigmoid(g)


def _ffn_head_kernel(x_hbm, gain_ref, wg_ref, wu_ref, wd_ref, o_hbm, wg_hbm, wu_hbm, wd_hbm,
                     acc_ref, h_ref, wg_bf, wu_bf, wd_bf, sem, w_sem,
                     *, n_steps, tf, row_chunk, row_part, col_chunk):
    j = pl.program_id(0)
    tm, d = acc_ref.shape

    def weight_copies(step):
        cols = pl.ds(pl.multiple_of(step * tf, tf), tf)
        return (pltpu.make_async_copy(wg_bf, wg_hbm.at[:, cols], w_sem.at[0]),
                pltpu.make_async_copy(wu_bf, wu_hbm.at[:, cols], w_sem.at[1]),
                pltpu.make_async_copy(wd_bf, wd_hbm.at[cols, :], w_sem.at[2]))

    @pl.when(j == 0)
    def _():
        copy = pltpu.make_async_copy(x_hbm.at[pl.ds(0, tm), :], acc_ref, sem)
        copy.start()
        copy.wait()
        for r in range(0, tm, row_chunk):
            rows = slice(r, r + row_chunk)
            h_ref[rows, :] = _rmsnorm(acc_ref[rows, :], gain_ref[...]).astype(_BF16)

    @pl.when(j > 0)
    def _():
        for copy in weight_copies(j - 1):
            copy.wait()

    wg_bf[...] = wg_ref[...].astype(_BF16)
    wu_bf[...] = wu_ref[...].astype(_BF16)
    wd_bf[...] = wd_ref[...].astype(_BF16)
    for copy in weight_copies(j):
        copy.start()

    parts = [slice(r, r + row_part) for r in range(0, tm, row_part)]
    acts = []
    for rows in parts:
        h = h_ref[rows, :]
        g = jnp.dot(h, wg_bf[...], preferred_element_type=_F32)
        u = jnp.dot(h, wu_bf[...], preferred_element_type=_F32)
        acts.append((_silu(g) * u * 0.5).astype(_BF16))
    for rows, a in zip(parts, acts):
        for c in range(0, d, col_chunk):
            cols = slice(c, c + col_chunk)
            acc_ref[rows, cols] += jnp.dot(a, wd_bf[:, cols], preferred_element_type=_F32)

    @pl.when(j == n_steps - 1)
    def _():
        for copy in weight_copies(j):
            copy.wait()
        copy = pltpu.make_async_copy(acc_ref, o_hbm, sem)
        copy.start()
        copy.wait()


def _ffn_head(x, gain, wg, wu, wd, *, rows, tf):
    s, d = x.shape
    f = wg.shape[1]
    assert f % tf == 0 and rows % 256 == 0
    n_steps = f // tf
    any_spec = pl.BlockSpec(memory_space=pl.ANY)
    return pl.pallas_call(
        functools.partial(_ffn_head_kernel, n_steps=n_steps, tf=tf, row_chunk=128, row_part=256,
                          col_chunk=1024),
        grid=(n_steps,),
        in_specs=[
            any_spec,
            pl.BlockSpec((1, d), lambda j: (0, 0)),
            pl.BlockSpec((d, tf), lambda j: (0, j)),
            pl.BlockSpec((d, tf), lambda j: (0, j)),
            pl.BlockSpec((tf, d), lambda j: (j, 0)),
        ],
        out_specs=[any_spec, any_spec, any_spec, any_spec],
        out_shape=[jax.ShapeDtypeStruct((rows, d), _F32), jax.ShapeDtypeStruct(wg.shape, _BF16),
                   jax.ShapeDtypeStruct(wu.shape, _BF16), jax.ShapeDtypeStruct(wd.shape, _BF16)],
        scratch_shapes=[pltpu.VMEM((rows, d), _F32), pltpu.VMEM((rows, d), _BF16),
                        pltpu.VMEM((d, tf), _BF16), pltpu.VMEM((d, tf), _BF16), pltpu.VMEM((tf, d), _BF16),
                        pltpu.SemaphoreType.DMA, pltpu.SemaphoreType.DMA((3,))],
        compiler_params=_params("arbitrary"),
        name="ffn_head",
    )(x, gain, wg, wu, wd)


def _ffn_kernel(x_hbm, gain_ref, wg_ref, wu_ref, wd_ref, fgain_ref, *rest, head_blocks,
                n_casts, n_steps, n_row_blocks, tf, tail_subs, final_norm, row_chunk, col_chunk):
    if head_blocks:
        head_hbm, rest = rest[0], rest[1:]
    cast_in = rest[:n_casts]
    o_ref = rest[n_casts]
    cast_out = rest[n_casts + 1:2 * n_casts + 1]
    x_buf, h_ref, x_sem = rest[2 * n_casts + 1:]
    i = pl.program_id(0)
    j = pl.program_id(1)
    tm, d = o_ref.shape
    computed = i >= head_blocks

    def x_copy(block):
        return pltpu.make_async_copy(x_hbm.at[pl.ds(block * tm, tm), :], x_buf, x_sem)

    def activation(k):
        cols = slice(k * tf, (k + 1) * tf)
        h = h_ref[...]
        g = jnp.dot(h, wg_ref[:, cols], preferred_element_type=_F32)
        u = jnp.dot(h, wu_ref[:, cols], preferred_element_type=_F32)
        return (_silu(g) * u * 0.5).astype(_BF16)

    def cast_blocks():
        for src, dst in zip(cast_in, cast_out):
            dst[...] = src[...].astype(dst.dtype)

    def step(n_subs):
        acts = [activation(k) for k in range(n_subs)]
        cast_blocks()
        for k, a in enumerate(acts):
            for c in range(0, d, col_chunk):
                cols = slice(c, c + col_chunk)
                o_ref[:, cols] += jnp.dot(a, wd_ref[k * tf:(k + 1) * tf, cols],
                                          preferred_element_type=_F32)

    if head_blocks:
        @pl.when(i < head_blocks)
        def _():
            @pl.when(j == 0)
            def _():
                copy = pltpu.make_async_copy(head_hbm.at[pl.ds(i * tm, tm), :], x_buf, x_sem)
                copy.start()
                copy.wait()
                for r in range(0, tm, row_chunk):
                    rows = slice(r, r + row_chunk)
                    y = x_buf[rows, :]
                    o_ref[rows, :] = _rmsnorm(y, fgain_ref[...]) if final_norm else y

            cast_blocks()

    @pl.when(jnp.logical_and(j == 0, computed))
    def _():
        if not head_blocks:
            @pl.when(i == 0)
            def _():
                x_copy(0).start()

        x_copy(i).wait()
        for r in range(0, tm, row_chunk):
            rows = slice(r, r + row_chunk)
            xr = x_buf[rows, :]
            h_ref[rows, :] = _rmsnorm(xr, gain_ref[...]).astype(_BF16)
            o_ref[rows, :] = xr
        step(2)

    @pl.when(j == 1)
    def _():
        @pl.when(jnp.logical_and(i + 1 < n_row_blocks, i + 1 >= head_blocks))
        def _():
            x_copy(i + 1).start()

    @pl.when(jnp.logical_and(jnp.logical_and(j > 0, j < n_steps - 1), computed))
    def _():
        step(2)

    @pl.when(jnp.logical_and(j == n_steps - 1, computed))
    def _():
        step(tail_subs)
        if final_norm:
            for r in range(0, tm, row_chunk):
                rows = slice(r, r + row_chunk)
                o_ref[rows, :] = _rmsnorm(o_ref[rows, :], fgain_ref[...])


_CAST_ROWS = 16


def _cast_specs(w, n_steps, total_steps):
    rows, width = w.shape
    assert rows % _CAST_ROWS == 0
    splits = [k for k in (4, 2, 1)
              if width % (k * 128) == 0 and (rows // _CAST_ROWS) * k <= total_steps]
    if not splits:
        return None
    k = splits[0]
    n_blocks = (rows // _CAST_ROWS) * k

    def index(i, j):
        b = jnp.minimum(i * n_steps + j, n_blocks - 1)
        return b // k, b % k

    block = (_CAST_ROWS, width // k)
    return pl.BlockSpec(block, index), pl.BlockSpec(block, index), jax.ShapeDtypeStruct(w.shape, _BF16)


def _ffn(x, gain, wg, wu, wd, fgain, *, final_norm, casts=(), tm=512, tf=256):
    s, d = x.shape
    f = wg.shape[1]
    assert s % tm == 0 and f % tf == 0
    n_steps = pl.cdiv(f, 2 * tf)
    assert n_steps >= 3
    tail_subs = (f - (n_steps - 1) * 2 * tf) // tf
    n_row_blocks = s // tm
    head_blocks = min(2, n_row_blocks - 1) if wg.dtype != _BF16 else 0
    head = ()
    if head_blocks:
        head_rows, wg, wu, wd = _ffn_head(x, gain, wg, wu, wd, rows=head_blocks * tm, tf=tf)
        head = (head_rows,)
    w_col = lambda i, j: jnp.where(i < head_blocks, 0, j)
    all_specs = [_cast_specs(w, n_steps, n_row_blocks * n_steps) for w in casts]
    hosted = [w for w, spec in zip(casts, all_specs) if spec is not None]
    cast_specs = [spec for spec in all_specs if spec is not None]
    kern = functools.partial(
        _ffn_kernel, head_blocks=head_blocks, n_casts=len(hosted), n_steps=n_steps, n_row_blocks=n_row_blocks,
        tf=tf, tail_subs=tail_subs, final_norm=final_norm, row_chunk=128, col_chunk=1024)
    out, *hosted_bf = pl.pallas_call(
        kern,
        grid=(n_row_blocks, n_steps),
        in_specs=[
            pl.BlockSpec(memory_space=pl.ANY),
            pl.BlockSpec((1, d), lambda i, j: (0, 0)),
            pl.BlockSpec((d, 2 * tf), lambda i, j: (0, w_col(i, j))),
            pl.BlockSpec((d, 2 * tf), lambda i, j: (0, w_col(i, j))),
            pl.BlockSpec((2 * tf, d), lambda i, j: (w_col(i, j), 0)),
            pl.BlockSpec((1, d), lambda i, j: (0, 0)),
        ] + [pl.BlockSpec(memory_space=pl.ANY)] * len(head) + [spec[0] for spec in cast_specs],
        out_specs=[pl.BlockSpec((tm, d), lambda i, j: (i, 0))] + [spec[1] for spec in cast_specs],
        out_shape=[jax.ShapeDtypeStruct((s, d), _F32)] + [spec[2] for spec in cast_specs],
        scratch_shapes=[pltpu.VMEM((tm, d), _F32), pltpu.VMEM((tm, d), _BF16),
                        pltpu.SemaphoreType.DMA],
        compiler_params=_params("arbitrary", "arbitrary"),
        name="ffn",
    )(x, gain, wg.astype(_BF16), wu.astype(_BF16), wd.astype(_BF16), fgain, *head, *hosted)
    hosted_bf = iter(hosted_bf)
    return [out] + [next(hosted_bf) if spec is not None else w.astype(_BF16)
                    for w, spec in zip(casts, all_specs)]


def _proj_in_kernel(x_hbm, gain_ref, pos_ref, freq_ref, w_ref, zr_ref, zf_ref,
                    x_buf, h_ref, cos_ref, sin_ref, x_sem,
                    *, n_row_blocks, n_plain, n_q_blocks, heads_per_block, head_dim, k_scale, row_chunk):
    i = pl.program_id(0)
    j = pl.program_id(1)
    tm = x_buf.shape[0]
    half = head_dim // 2

    def x_copy(block):
        return pltpu.make_async_copy(x_hbm.at[pl.ds(block * tm, tm), :], x_buf, x_sem)

    def head_dot(t):
        cols = slice(t * head_dim, (t + 1) * head_dim)
        return jnp.dot(h_ref[...], w_ref[:, cols], preferred_element_type=_F32)

    def plain_step():
        for t in range(heads_per_block):
            zf_ref[:, t * head_dim:(t + 1) * head_dim] = head_dot(t)

    @pl.when(j == 0)
    def _():
        @pl.when(i == 0)
        def _():
            x_copy(0).start()

        x_copy(i).wait()
        for r in range(0, tm, row_chunk):
            rows = slice(r, r + row_chunk)
            h_ref[rows, :] = _rmsnorm(x_buf[rows, :], gain_ref[...]).astype(_BF16)
        ang = pos_ref[...].astype(_F32) * freq_ref[...]
        cos_ref[...] = jnp.cos(ang)
        sin_ref[...] = jnp.sin(ang)
        plain_step()

    @pl.when(j == 1)
    def _():
        @pl.when(i + 1 < n_row_blocks)
        def _():
            x_copy(i + 1).start()

    @pl.when(jnp.logical_and(j > 0, j < n_plain))
    def _():
        plain_step()

    @pl.when(j >= n_plain)
    def _():
        cos = cos_ref[...]
        sin = sin_ref[...]
        scale = jnp.where(j - n_plain >= n_q_blocks, k_scale, 1.0).astype(_F32)
        for t in range(heads_per_block):
            z = head_dot(t)
            t1 = z[:, :half]
            t2 = z[:, half:]
            lo = t * head_dim
            zr_ref[:, lo:lo + half] = ((t1 * cos - t2 * sin) * scale).astype(zr_ref.dtype)
            zr_ref[:, lo + half:lo + head_dim] = ((t1 * sin + t2 * cos) * scale).astype(zr_ref.dtype)


def _proj_in(x, gain, pos, freqs, w, *, ret_dim, head_dim, tm=1024, tn=1024):
    s, d = x.shape
    n = w.shape[1]
    assert s % tm == 0 and n % tn == 0 and ret_dim % tn == 0 and tn % head_dim == 0
    half = head_dim // 2
    n_rot = 2 * ret_dim // tn
    n_plain = n // tn - n_rot
    assert n_plain >= 2
    kern = functools.partial(
        _proj_in_kernel, n_row_blocks=s // tm, n_plain=n_plain, n_q_blocks=ret_dim // tn,
        heads_per_block=tn // head_dim, head_dim=head_dim, k_scale=head_dim ** -0.5, row_chunk=128)
    return pl.pallas_call(
        kern,
        grid=(s // tm, n // tn),
        in_specs=[
            pl.BlockSpec(memory_space=pl.ANY),
            pl.BlockSpec((1, d), lambda i, j: (0, 0)),
            pl.BlockSpec((tm, 1), lambda i, j: (i, 0)),
            pl.BlockSpec((1, half), lambda i, j: (0, 0)),
            pl.BlockSpec((d, tn), lambda i, j: (0, jnp.where(j < n_plain, j + n_rot, j - n_plain))),
        ],
        out_specs=[
            pl.BlockSpec((tm, tn), lambda i, j: (i, jnp.maximum(j - n_plain, 0))),
            pl.BlockSpec((tm, tn), lambda i, j: (i, jnp.minimum(j, n_plain - 1))),
        ],
        out_shape=[jax.ShapeDtypeStruct((s, n_rot * tn), _BF16),
                   jax.ShapeDtypeStruct((s, n_plain * tn), _F32)],
        scratch_shapes=[pltpu.VMEM((tm, d), _F32), pltpu.VMEM((tm, d), _BF16),
                        pltpu.VMEM((tm, half), _F32), pltpu.VMEM((tm, half), _F32),
                        pltpu.SemaphoreType.DMA],
        compiler_params=_params("arbitrary", "arbitrary"),
        name="proj_in",
    )(x, gain, pos, freqs, w)


def _contract_rows(a, b):
    return lax.dot_general(a, b, (((0,), (0,)), ((), ())), preferred_element_type=_F32)


def _ret_state_kernel(lgb_ref, k_ref, v_ref, b_ref, state_ref, vdec_ref, cdec_ref, *, chunk, heads, hd):
    t = pl.program_id(0)

    @pl.when(t == 0)
    def _():
        state_ref[...] = jnp.zeros_like(state_ref)
        idx = lax.broadcasted_iota(jnp.int32, (chunk, hd), 0).astype(_F32)
        for h in range(heads):
            lgb = jax.nn.log_sigmoid(lgb_ref[h])
            vdec_ref[h] = jnp.exp(idx * lgb)
            cdec_ref[h] = jnp.broadcast_to(jnp.exp(chunk * lgb), (1, hd))

    for h in range(heads):
        cols = slice(h * hd, (h + 1) * hd)
        b_ref[h, 0] = state_ref[h].astype(_BF16)
        v_dec = (v_ref[:, cols] * vdec_ref[h]).astype(_BF16)
        state_ref[h] = state_ref[h] * cdec_ref[h] + _contract_rows(k_ref[:, cols], v_dec)


def _ret_out_kernel(lgf_ref, lgb_ref, q_ref, k_ref, v_ref, g_ref, b_ref, hn_ref, o_ref,
                    state_ref, dm_ref, dec_ref, cdec_ref, *, chunk, heads, hd):
    c = pl.program_id(0)

    @pl.when(c == 0)
    def _():
        state_ref[...] = jnp.zeros_like(state_ref)
        rel = (lax.broadcasted_iota(jnp.int32, (chunk, chunk), 0)
               - lax.broadcasted_iota(jnp.int32, (chunk, chunk), 1)).astype(_F32)
        idx = lax.broadcasted_iota(jnp.int32, (chunk, hd), 0).astype(_F32)
        for h in range(heads):
            lgf = jax.nn.log_sigmoid(lgf_ref[h])
            lgb = jax.nn.log_sigmoid(lgb_ref[h])
            dm_ref[h] = jnp.exp(jnp.where(rel >= 0, rel * lgf, -rel * lgb))
            dec_ref[0, h] = jnp.exp((idx + 1.0) * lgf)
            dec_ref[1, h] = jnp.exp((chunk - idx) * lgb)
            dec_ref[2, h] = jnp.exp((chunk - 1.0 - idx) * lgf)
            cdec_ref[h] = jnp.broadcast_to(jnp.exp(chunk * lgf), (1, hd))

    for h in range(heads):
        cols = slice(h * hd, (h + 1) * hd)
        q = q_ref[:, cols]
        k = k_ref[:, cols]
        v = v_ref[:, cols]
        scores = lax.dot_general(q, k, (((1,), (1,)), ((), ())), preferred_element_type=_F32)
        scores = scores * dm_ref[h]
        out = jnp.dot(scores.astype(_BF16), v.astype(_BF16), preferred_element_type=_F32)
        out += jnp.dot(q, state_ref[h].astype(_BF16), preferred_element_type=_F32) * dec_ref[0, h]
        out += jnp.dot(q, b_ref[h, 0], preferred_element_type=_F32) * dec_ref[1, h]

        out = out * lax.rsqrt(jnp.mean(out * out, axis=-1, keepdims=True) + EPS)
        out = out * hn_ref[:, cols]
        o_ref[:, cols] = (out * _silu(g_ref[:, cols])).astype(o_ref.dtype)

        v_dec = (v * dec_ref[2, h]).astype(_BF16)
        state_ref[h] = state_ref[h] * cdec_ref[h] + _contract_rows(k, v_dec)


def _retention(zr, zf, logit_f, logit_b, head_gain, *, heads, head_dim, chunk=256):
    s = zr.shape[0]
    assert s % chunk == 0
    nc = s // chunk
    hd = head_dim
    rd = heads * hd
    decay_spec = pl.BlockSpec((heads, 1, 1), lambda c: (0, 0, 0))

    b_states = pl.pallas_call(
        functools.partial(_ret_state_kernel, chunk=chunk, heads=heads, hd=hd),
        grid=(nc,),
        in_specs=[
            decay_spec,
            pl.BlockSpec((chunk, rd), lambda t: (nc - 1 - t, 1)),
            pl.BlockSpec((chunk, rd), lambda t: (nc - 1 - t, 0)),
        ],
        out_specs=pl.BlockSpec((heads, 1, hd, hd), lambda t: (0, nc - 1 - t, 0, 0)),
        out_shape=jax.ShapeDtypeStruct((heads, nc, hd, hd), _BF16),
        scratch_shapes=[pltpu.VMEM((heads, hd, hd), _F32), pltpu.VMEM((heads, chunk, hd), _F32),
                        pltpu.VMEM((heads, 1, hd), _F32)],
        compiler_params=_params("arbitrary"),
        name="ret_state",
    )(logit_b, zr, zf)

    return pl.pallas_call(
        functools.partial(_ret_out_kernel, chunk=chunk, heads=heads, hd=hd),
        grid=(nc,),
        in_specs=[
            decay_spec,
            decay_spec,
            pl.BlockSpec((chunk, rd), lambda c: (c, 0)),
            pl.BlockSpec((chunk, rd), lambda c: (c, 1)),
            pl.BlockSpec((chunk, rd), lambda c: (c, 0)),
            pl.BlockSpec((chunk, rd), lambda c: (c, 1)),
            pl.BlockSpec((heads, 1, hd, hd), lambda c: (0, c, 0, 0)),
            pl.BlockSpec((1, rd), lambda c: (0, 0)),
        ],
        out_specs=pl.BlockSpec((chunk, rd), lambda c: (c, 0)),
        out_shape=jax.ShapeDtypeStruct((s, rd), _BF16),
        scratch_shapes=[pltpu.VMEM((heads, hd, hd), _F32), pltpu.VMEM((heads, chunk, chunk), _F32),
                        pltpu.VMEM((3, heads, chunk, hd), _F32), pltpu.VMEM((heads, 1, hd), _F32)],
        compiler_params=_params("arbitrary"),
        name="ret_out",
    )(logit_f, logit_b, zr, zr, zf, zf, b_states, head_gain)


def _window_sum(p, window):
    rows = p.shape[0]
    assert window >= 2 and window & (window - 1) == 0
    shifted = lambda a, k: pltpu.roll(a, k % rows, axis=0)
    s = shifted(p, 1) + p
    width = 2
    while width < window:
        s = shifted(s, width // 2) + shifted(s, -(width // 2))
        width *= 2
    return s


def _pool_kernel(prev_ref, cur_ref, next_ref, w_ref, scale_ref, o_ref, pad_ref, *, tm, seq, group_dim):
    i = pl.program_id(0)
    n_blocks = pl.num_programs(0)
    halo = POOL_HALO
    pad_ref[0:halo, :] = jnp.where(i > 0, prev_ref[...], 0.0)
    pad_ref[halo:halo + tm, :] = cur_ref[...]
    pad_ref[halo + tm:, :] = jnp.where(i < n_blocks - 1, next_ref[...], 0.0)

    row = i * tm + lax.broadcasted_iota(jnp.int32, (tm, 1), 0)
    for gi, window in enumerate(POOL_WINDOWS):
        lo = window // 2
        hi = window - 1 - lo
        cols = slice(gi * group_dim, (gi + 1) * group_dim)
        total = _window_sum(pad_ref[:, cols], window)[halo:halo + tm]
        count = (jnp.minimum(row + hi + 1, seq) - jnp.maximum(row - lo, 0)).astype(_F32)
        pooled = total / count - cur_ref[:, cols]
        mapped = jnp.dot(pooled.astype(_BF16), w_ref[gi], preferred_element_type=_F32)
        o_ref[:, cols] = (mapped * scale_ref[:, cols]).astype(o_ref.dtype)


def _pool(z, pool_w, pool_scale, *, tm=512):
    s, n = z.shape
    groups, group_dim, _ = pool_w.shape
    pool_dim = groups * group_dim
    assert groups == len(POOL_WINDOWS) and max(POOL_WINDOWS) // 2 <= POOL_HALO
    assert s % tm == 0 and tm % POOL_HALO == 0 and n % pool_dim == 0
    col = n // pool_dim - 1
    r = tm // POOL_HALO
    last = s // POOL_HALO - 1
    return pl.pallas_call(
        functools.partial(_pool_kernel, tm=tm, seq=s, group_dim=group_dim),
        grid=(s // tm,),
        in_specs=[
            pl.BlockSpec((POOL_HALO, pool_dim), lambda i: (jnp.maximum(i * r - 1, 0), col)),
            pl.BlockSpec((tm, pool_dim), lambda i: (i, col)),
            pl.BlockSpec((POOL_HALO, pool_dim), lambda i: (jnp.minimum((i + 1) * r, last), col)),
            pl.BlockSpec((groups, group_dim, group_dim), lambda i: (0, 0, 0)),
            pl.BlockSpec((1, pool_dim), lambda i: (0, 0)),
        ],
        out_specs=pl.BlockSpec((tm, pool_dim), lambda i: (i, 0)),
        out_shape=jax.ShapeDtypeStruct((s, pool_dim), _BF16),
        scratch_shapes=[pltpu.VMEM((tm + 2 * POOL_HALO, pool_dim), _F32)],
        compiler_params=_params("parallel"),
        name="pool",
    )(z, z, z, pool_w, pool_scale)


def _proj_out_kernel(x_ref, ret_ref, pool_ref, wr_ref, wp_ref, o_ref):
    acc = jnp.dot(ret_ref[...], wr_ref[...], preferred_element_type=_F32)
    acc += jnp.dot(pool_ref[...], wp_ref[...], preferred_element_type=_F32)
    o_ref[...] = x_ref[...] + acc


def _proj_out(x, ret, pool, w, *, tm=1024, tn=1024):
    s, d = x.shape
    kr = ret.shape[1]
    kp = pool.shape[1]
    assert kr == kp and w.shape == (kr + kp, d) and s % tm == 0 and d % tn == 0
    return pl.pallas_call(
        _proj_out_kernel,
        grid=(s // tm, d // tn),
        in_specs=[
            pl.BlockSpec((tm, tn), lambda i, j: (i, j)),
            pl.BlockSpec((tm, kr), lambda i, j: (i, 0)),
            pl.BlockSpec((tm, kp), lambda i, j: (i, 0)),
            pl.BlockSpec((kr, tn), lambda i, j: (0, j)),
            pl.BlockSpec((kp, tn), lambda i, j: (1, j)),
        ],
        out_specs=pl.BlockSpec((tm, tn), lambda i, j: (i, j)),
        out_shape=jax.ShapeDtypeStruct((s, d), _F32),
        compiler_params=_params("parallel", "arbitrary"),
        name="proj_out",
    )(x, ret, pool, w, w)


def kernel(x, positions, ffn1_norm, ffn1_w_gate, ffn1_w_up, ffn1_w_down, mix_norm, w_in,
           ret_decay_fwd, ret_decay_bwd, ret_head_norm, pool_w, pool_scale, w_out,
           ffn2_norm, ffn2_w_gate, ffn2_w_up, ffn2_w_down, final_norm):
    b, s, d = x.shape
    assert b == 1, "retention scan is written for a single sequence"
    depth = ffn1_norm.shape[0]
    heads = ret_decay_fwd.shape[1]
    ret_dim = ret_head_norm.shape[1]
    head_dim = ret_dim // heads

    freqs = 1.0 / (ROPE_BASE ** (jnp.arange(0, head_dim, 2, dtype=_F32) / head_dim))
    freqs = freqs.reshape(1, head_dim // 2)
    pos = positions.reshape(s, 1)
    final_gain = final_norm.reshape(1, d)
    row = lambda v: v.reshape(1, -1).astype(_F32)
    bf = lambda w: w.astype(_BF16)

    xs = x.reshape(s, d)
    for layer in range(depth):
        last = layer == depth - 1
        xs, w_in_bf, w_out_bf = _ffn(
            xs, row(ffn1_norm[layer]), ffn1_w_gate[layer], ffn1_w_up[layer], ffn1_w_down[layer],
            final_gain, final_norm=False, casts=(w_in[layer], w_out[layer]))
        zr, zf = _proj_in(xs, row(mix_norm[layer]), pos, freqs, w_in_bf,
                          ret_dim=ret_dim, head_dim=head_dim)
        ret = _retention(zr, zf, ret_decay_fwd[layer].reshape(heads, 1, 1).astype(_F32),
                         ret_decay_bwd[layer].reshape(heads, 1, 1).astype(_F32),
                         row(ret_head_norm[layer]), heads=heads, head_dim=head_dim)
        pool = _pool(zf, bf(pool_w[layer]), row(pool_scale[layer]))
        xs = _proj_out(xs, ret, pool, w_out_bf)
        xs, = _ffn(xs, row(ffn2_norm[layer]), ffn2_w_gate[layer], ffn2_w_up[layer], ffn2_w_down[layer],
                   final_gain, final_norm=last)
    return xs.reshape(b, s, d)
```

```python
import functools

import jax
import jax.numpy as jnp
from jax import lax
from jax.experimental import pallas as pl
from jax.experimental.pallas import tpu as pltpu

EPS = 1e-6
ROPE_BASE = 10000.0
POOL_WINDOWS = (2, 4, 8, 16)
POOL_HALO = 16

_F32 = jnp.float32
_BF16 = jnp.bfloat16
_VMEM_LIMIT_BYTES = 62 * 1024 * 1024


def _params(*semantics):
    return pltpu.CompilerParams(dimension_semantics=semantics, vmem_limit_bytes=_VMEM_LIMIT_BYTES)


def _rmsnorm(x, gain):
    return x * lax.rsqrt(jnp.mean(x * x, axis=-1, keepdims=True) + EPS) * gain


def _silu(g):
    return g * jax.nn.sigmoid(g)


def _ffn_head_kernel(x_hbm, gain_ref, wg_ref, wu_ref, wd_ref, o_hbm, wg_hbm, wu_hbm, wd_hbm,
                     acc_ref, h_ref, wg_bf, wu_bf, wd_bf, sem, w_sem,
                     *, n_steps, tf, row_chunk, row_part, col_chunk):
    j = pl.program_id(0)
    tm, d = acc_ref.shape

    def weight_copies(step):
        cols = pl.ds(pl.multiple_of(step * tf, tf), tf)
        return (pltpu.make_async_copy(wg_bf, wg_hbm.at[:, cols], w_sem.at[0]),
                pltpu.make_async_copy(wu_bf, wu_hbm.at[:, cols], w_sem.at[1]),
                pltpu.make_async_copy(wd_bf, wd_hbm.at[cols, :], w_sem.at[2]))

    @pl.when(j == 0)
    def _():
        copy = pltpu.make_async_copy(x_hbm.at[pl.ds(0, tm), :], acc_ref, sem)
        copy.start()
        copy.wait()
        for r in range(0, tm, row_chunk):
            rows = slice(r, r + row_chunk)
            h_ref[rows, :] = _rmsnorm(acc_ref[rows, :], gain_ref[...]).astype(_BF16)

    @pl.when(j > 0)
    def _():
        for copy in weight_copies(j - 1):
            copy.wait()

    wg_bf[...] = wg_ref[...].astype(_BF16)
    wu_bf[...] = wu_ref[...].astype(_BF16)
    wd_bf[...] = wd_ref[...].astype(_BF16)
    for copy in weight_copies(j):
        copy.start()

    parts = [slice(r, r + row_part) for r in range(0, tm, row_part)]
    acts = []
    for rows in parts:
        h = h_ref[rows, :]
        g = jnp.dot(h, wg_bf[...], preferred_element_type=_F32)
        u = jnp.dot(h, wu_bf[...], preferred_element_type=_F32)
        acts.append((_silu(g) * u * 0.5).astype(_BF16))
    for rows, a in zip(parts, acts):
        for c in range(0, d, col_chunk):
            cols = slice(c, c + col_chunk)
            acc_ref[rows, cols] += jnp.dot(a, wd_bf[:, cols], preferred_element_type=_F32)

    @pl.when(j == n_steps - 1)
    def _():
        for copy in weight_copies(j):
            copy.wait()
        copy = pltpu.make_async_copy(acc_ref, o_hbm, sem)
        copy.start()
        copy.wait()


def _ffn_head(x, gain, wg, wu, wd, *, rows, tf):
    s, d = x.shape
    f = wg.shape[1]
    assert f % tf == 0 and rows % 256 == 0
    n_steps = f // tf
    any_spec = pl.BlockSpec(memory_space=pl.ANY)
    return pl.pallas_call(
        functools.partial(_ffn_head_kernel, n_steps=n_steps, tf=tf, row_chunk=128, row_part=256,
                          col_chunk=1024),
        grid=(n_steps,),
        in_specs=[
            any_spec,
            pl.BlockSpec((1, d), lambda j: (0, 0)),
            pl.BlockSpec((d, tf), lambda j: (0, j)),
            pl.BlockSpec((d, tf), lambda j: (0, j)),
            pl.BlockSpec((tf, d), lambda j: (j, 0)),
        ],
        out_specs=[any_spec, any_spec, any_spec, any_spec],
        out_shape=[jax.ShapeDtypeStruct((rows, d), _F32), jax.ShapeDtypeStruct(wg.shape, _BF16),
                   jax.ShapeDtypeStruct(wu.shape, _BF16), jax.ShapeDtypeStruct(wd.shape, _BF16)],
        scratch_shapes=[pltpu.VMEM((rows, d), _F32), pltpu.VMEM((rows, d), _BF16),
                        pltpu.VMEM((d, tf), _BF16), pltpu.VMEM((d, tf), _BF16), pltpu.VMEM((tf, d), _BF16),
                        pltpu.SemaphoreType.DMA, pltpu.SemaphoreType.DMA((3,))],
        compiler_params=_params("arbitrary"),
        name="ffn_head",
    )(x, gain, wg, wu, wd)


def _ffn_kernel(x_hbm, gain_ref, wg_ref, wu_ref, wd_ref, fgain_ref, *rest, head_blocks, n_subs,
                n_casts, n_steps, n_row_blocks, tf, tail_subs, final_norm, row_chunk, col_chunk):
    if head_blocks:
        head_hbm, rest = rest[0], rest[1:]
    cast_in = rest[:n_casts]
    o_hbm = rest[n_casts]
    cast_out = rest[n_casts + 1:2 * n_casts + 1]
    acc_ref, h_ref, x_sem, o_sem = rest[2 * n_casts + 1:]
    i = pl.program_id(0)
    j = pl.program_id(1)
    _, tm, d = acc_ref.shape
    slot = i % 2
    other = 1 - slot
    computed = i >= head_blocks

    def x_copy(block, s):
        return pltpu.make_async_copy(x_hbm.at[pl.ds(block * tm, tm), :], acc_ref.at[s], x_sem.at[s])

    def o_copy(block, s):
        return pltpu.make_async_copy(acc_ref.at[s], o_hbm.at[pl.ds(block * tm, tm), :], o_sem.at[s])

    def row_chunks(body):
        def run(r, carry):
            body(pl.ds(pl.multiple_of(r * row_chunk, row_chunk), row_chunk))
            return carry
        lax.fori_loop(0, tm // row_chunk, run, 0)

    def final_rows():
        def body(rows):
            acc_ref[slot, rows, :] = _rmsnorm(acc_ref[slot, rows, :], fgain_ref[...])
        row_chunks(body)

    def activation(k):
        cols = slice(k * tf, (k + 1) * tf)
        g = jnp.dot(h_ref[...], wg_ref[:, cols], preferred_element_type=_F32)
        u = jnp.dot(h_ref[...], wu_ref[:, cols], preferred_element_type=_F32)
        return (_silu(g) * u * 0.5).astype(_BF16)

    def cast_blocks():
        for src, dst in zip(cast_in, cast_out):
            dst[...] = src[...].astype(dst.dtype)

    def step(subs):
        cast_blocks()
        for base in range(0, subs, 2):
            pair = range(base, min(base + 2, subs))
            acts = [activation(k) for k in pair]
            for k, a in zip(pair, acts):
                for c in range(0, d, col_chunk):
                    cols = slice(c, c + col_chunk)
                    acc_ref[slot, :, cols] += jnp.dot(a, wd_ref[k * tf:(k + 1) * tf, cols],
                                                      preferred_element_type=_F32)

    if head_blocks:
        @pl.when(i < head_blocks)
        def _():
            @pl.when(j == 0)
            def _():
                copy = pltpu.make_async_copy(head_hbm.at[pl.ds(i * tm, tm), :], acc_ref.at[slot],
                                             x_sem.at[slot])
                copy.start()
                copy.wait()
                if final_norm:
                    final_rows()
                o_copy(i, slot).start()

            cast_blocks()

    @pl.when(jnp.logical_and(j == 0, computed))
    def _():
        if not head_blocks:
            @pl.when(i == 0)
            def _():
                x_copy(0, 0).start()

        x_copy(i, slot).wait()

        def normalise(rows):
            h_ref[rows, :] = _rmsnorm(acc_ref[slot, rows, :], gain_ref[...]).astype(_BF16)
        row_chunks(normalise)
        step(n_subs)

    @pl.when(j == 2)
    def _():
        @pl.when(i > 0)
        def _():
            o_copy(i - 1, other).wait()

        @pl.when(jnp.logical_and(i + 1 < n_row_blocks, i + 1 >= head_blocks))
        def _():
            x_copy(i + 1, other).start()

    @pl.when(jnp.logical_and(jnp.logical_and(j > 0, j < n_steps - 1), computed))
    def _():
        step(n_subs)

    @pl.when(jnp.logical_and(j == n_steps - 1, computed))
    def _():
        step(tail_subs)
        if final_norm:
            final_rows()
        o_copy(i, slot).start()

    @pl.when(jnp.logical_and(j == n_steps - 1, i == n_row_blocks - 1))
    def _():
        o_copy(i, slot).wait()


_CAST_ROWS = 16


def _cast_specs(w, n_steps, total_steps):
    rows, width = w.shape
    assert rows % _CAST_ROWS == 0
    splits = [k for k in (4, 2, 1)
              if width % (k * 128) == 0 and (rows // _CAST_ROWS) * k <= total_steps]
    if not splits:
        return None
    k = splits[0]
    n_blocks = (rows // _CAST_ROWS) * k

    def index(i, j):
        b = jnp.minimum(i * n_steps + j, n_blocks - 1)
        return b // k, b % k

    block = (_CAST_ROWS, width // k)
    return pl.BlockSpec(block, index), pl.BlockSpec(block, index), jax.ShapeDtypeStruct(w.shape, _BF16)


def _ffn(x, gain, wg, wu, wd, fgain, *, final_norm, casts=(), tm=512, tf=256):
    s, d = x.shape
    f = wg.shape[1]
    assert s % tm == 0 and f % tf == 0
    n_subs = 2 if casts else 3
    n_steps = pl.cdiv(f, n_subs * tf)
    assert n_steps >= 3
    tail_subs = (f - (n_steps - 1) * n_subs * tf) // tf
    n_row_blocks = s // tm
    head_blocks = min(2, n_row_blocks - 1) if wg.dtype != _BF16 else 0
    head = ()
    if head_blocks:
        head_rows, wg, wu, wd = _ffn_head(x, gain, wg, wu, wd, rows=head_blocks * tm, tf=tf)
        head = (head_rows,)
    w_col = lambda i, j: jnp.where(i < head_blocks, 0, j)
    all_specs = [_cast_specs(w, n_steps, n_row_blocks * n_steps) for w in casts]
    hosted = [w for w, spec in zip(casts, all_specs) if spec is not None]
    cast_specs = [spec for spec in all_specs if spec is not None]
    kern = functools.partial(
        _ffn_kernel, head_blocks=head_blocks, n_subs=n_subs, n_casts=len(hosted), n_steps=n_steps,
        n_row_blocks=n_row_blocks, tf=tf, tail_subs=tail_subs, final_norm=final_norm, row_chunk=128, col_chunk=512)
    out, *hosted_bf = pl.pallas_call(
        kern,
        grid=(n_row_blocks, n_steps),
        in_specs=[
            pl.BlockSpec(memory_space=pl.ANY),
            pl.BlockSpec((1, d), lambda i, j: (0, 0)),
            pl.BlockSpec((d, n_subs * tf), lambda i, j: (0, w_col(i, j))),
            pl.BlockSpec((d, n_subs * tf), lambda i, j: (0, w_col(i, j))),
            pl.BlockSpec((n_subs * tf, d), lambda i, j: (w_col(i, j), 0)),
            pl.BlockSpec((1, d), lambda i, j: (0, 0)),
        ] + [pl.BlockSpec(memory_space=pl.ANY)] * len(head) + [spec[0] for spec in cast_specs],
        out_specs=[pl.BlockSpec(memory_space=pl.ANY)] + [spec[1] for spec in cast_specs],
        out_shape=[jax.ShapeDtypeStruct((s, d), _F32)] + [spec[2] for spec in cast_specs],
        scratch_shapes=[pltpu.VMEM((2, tm, d), _F32), pltpu.VMEM((tm, d), _BF16),
                        pltpu.SemaphoreType.DMA((2,)), pltpu.SemaphoreType.DMA((2,))],
        compiler_params=_params("arbitrary", "arbitrary"),
        name="ffn",
    )(x, gain, wg.astype(_BF16), wu.astype(_BF16), wd.astype(_BF16), fgain, *head, *hosted)
    hosted_bf = iter(hosted_bf)
    return [out] + [next(hosted_bf) if spec is not None else w.astype(_BF16)
                    for w, spec in zip(casts, all_specs)]


def _proj_in_kernel(x_hbm, gain_ref, pos_ref, freq_ref, w_ref, zr_ref, zf_ref,
                    x_buf, h_ref, cos_ref, sin_ref, x_sem,
                    *, n_row_blocks, n_plain, n_q_blocks, heads_per_block, head_dim, k_scale, row_chunk):
    i = pl.program_id(0)
    j = pl.program_id(1)
    tm = x_buf.shape[0]
    half = head_dim // 2

    def x_copy(block):
        return pltpu.make_async_copy(x_hbm.at[pl.ds(block * tm, tm), :], x_buf, x_sem)

    def head_dot(t):
        cols = slice(t * head_dim, (t + 1) * head_dim)
        return jnp.dot(h_ref[...], w_ref[:, cols], preferred_element_type=_F32)

    def plain_step():
        for t in range(heads_per_block):
            zf_ref[:, t * head_dim:(t + 1) * head_dim] = head_dot(t)

    @pl.when(j == 0)
    def _():
        @pl.when(i == 0)
        def _():
            x_copy(0).start()

        x_copy(i).wait()
        for r in range(0, tm, row_chunk):
            rows = slice(r, r + row_chunk)
            h_ref[rows, :] = _rmsnorm(x_buf[rows, :], gain_ref[...]).astype(_BF16)
        ang = pos_ref[...].astype(_F32) * freq_ref[...]
        cos_ref[...] = jnp.cos(ang)
        sin_ref[...] = jnp.sin(ang)
        plain_step()

    @pl.when(j == 1)
    def _():
        @pl.when(i + 1 < n_row_blocks)
        def _():
            x_copy(i + 1).start()

    @pl.when(jnp.logical_and(j > 0, j < n_plain))
    def _():
        plain_step()

    @pl.when(j >= n_plain)
    def _():
        cos = cos_ref[...]
        sin = sin_ref[...]
        scale = jnp.where(j - n_plain >= n_q_blocks, k_scale, 1.0).astype(_F32)
        for t in range(heads_per_block):
            z = head_dot(t)
            t1 = z[:, :half]
            t2 = z[:, half:]
            lo = t * head_dim
            zr_ref[:, lo:lo + half] = ((t1 * cos - t2 * sin) * scale).astype(zr_ref.dtype)
            zr_ref[:, lo + half:lo + head_dim] = ((t1 * sin + t2 * cos) * scale).astype(zr_ref.dtype)


def _proj_in(x, gain, pos, freqs, w, *, ret_dim, head_dim, tm=1024, tn=1024):
    s, d = x.shape
    n = w.shape[1]
    assert s % tm == 0 and n % tn == 0 and ret_dim % tn == 0 and tn % head_dim == 0
    half = head_dim // 2
    n_rot = 2 * ret_dim // tn
    n_plain = n // tn - n_rot
    assert n_plain >= 2
    kern = functools.partial(
        _proj_in_kernel, n_row_blocks=s // tm, n_plain=n_plain, n_q_blocks=ret_dim // tn,
        heads_per_block=tn // head_dim, head_dim=head_dim, k_scale=head_dim ** -0.5, row_chunk=128)
    return pl.pallas_call(
        kern,
        grid=(s // tm, n // tn),
        in_specs=[
            pl.BlockSpec(memory_space=pl.ANY),
            pl.BlockSpec((1, d), lambda i, j: (0, 0)),
            pl.BlockSpec((tm, 1), lambda i, j: (i, 0)),
            pl.BlockSpec((1, half), lambda i, j: (0, 0)),
            pl.BlockSpec((d, tn), lambda i, j: (0, jnp.where(j < n_plain, j + n_rot, j - n_plain))),
        ],
        out_specs=[
            pl.BlockSpec((tm, tn), lambda i, j: (i, jnp.maximum(j - n_plain, 0))),
            pl.BlockSpec((tm, tn), lambda i, j: (i, jnp.minimum(j, n_plain - 1))),
        ],
        out_shape=[jax.ShapeDtypeStruct((s, n_rot * tn), _BF16),
                   jax.ShapeDtypeStruct((s, n_plain * tn), _F32)],
        scratch_shapes=[pltpu.VMEM((tm, d), _F32), pltpu.VMEM((tm, d), _BF16),
                        pltpu.VMEM((tm, half), _F32), pltpu.VMEM((tm, half), _F32),
                        pltpu.SemaphoreType.DMA],
        compiler_params=_params("arbitrary", "arbitrary"),
        name="proj_in",
    )(x, gain, pos, freqs, w)


def _contract_rows(a, b):
    return lax.dot_general(a, b, (((0,), (0,)), ((), ())), preferred_element_type=_F32)


def _ret_state_kernel(lgb_ref, k_ref, v_ref, b_ref, state_ref, vdec_ref, cdec_ref, *, chunk, heads, hd):
    t = pl.program_id(0)

    @pl.when(t == 0)
    def _():
        state_ref[...] = jnp.zeros_like(state_ref)
        idx = lax.broadcasted_iota(jnp.int32, (chunk, hd), 0).astype(_F32)
        for h in range(heads):
            lgb = jax.nn.log_sigmoid(lgb_ref[h])
            vdec_ref[h] = jnp.exp(idx * lgb)
            cdec_ref[h] = jnp.broadcast_to(jnp.exp(chunk * lgb), (1, hd))

    for h in range(heads):
        cols = slice(h * hd, (h + 1) * hd)
        b_ref[h, 0] = state_ref[h].astype(_BF16)
        v_dec = (v_ref[:, cols] * vdec_ref[h]).astype(_BF16)
        state_ref[h] = state_ref[h] * cdec_ref[h] + _contract_rows(k_ref[:, cols], v_dec)


def _ret_out_kernel(lgf_ref, lgb_ref, q_ref, k_ref, v_ref, g_ref, b_ref, hn_ref, o_ref,
                    state_ref, dm_ref, dec_ref, cdec_ref, *, chunk, heads, hd):
    c = pl.program_id(0)

    @pl.when(c == 0)
    def _():
        state_ref[...] = jnp.zeros_like(state_ref)
        rel = (lax.broadcasted_iota(jnp.int32, (chunk, chunk), 0)
               - lax.broadcasted_iota(jnp.int32, (chunk, chunk), 1)).astype(_F32)
        idx = lax.broadcasted_iota(jnp.int32, (chunk, hd), 0).astype(_F32)
        for h in range(heads):
            lgf = jax.nn.log_sigmoid(lgf_ref[h])
            lgb = jax.nn.log_sigmoid(lgb_ref[h])
            dm_ref[h] = jnp.exp(jnp.where(rel >= 0, rel * lgf, -rel * lgb))
            dec_ref[0, h] = jnp.exp((idx + 1.0) * lgf)
            dec_ref[1, h] = jnp.exp((chunk - idx) * lgb)
            dec_ref[2, h] = jnp.exp((chunk - 1.0 - idx) * lgf)
            cdec_ref[h] = jnp.broadcast_to(jnp.exp(chunk * lgf), (1, hd))

    for h in range(heads):
        cols = slice(h * hd, (h + 1) * hd)
        q = q_ref[:, cols]
        k = k_ref[:, cols]
        v = v_ref[:, cols]
        scores = lax.dot_general(q, k, (((1,), (1,)), ((), ())), preferred_element_type=_F32)
        scores = scores * dm_ref[h]
        out = jnp.dot(scores.astype(_BF16), v.astype(_BF16), preferred_element_type=_F32)
        out += jnp.dot(q, state_ref[h].astype(_BF16), preferred_element_type=_F32) * dec_ref[0, h]
        out += jnp.dot(q, b_ref[h, 0], preferred_element_type=_F32) * dec_ref[1, h]

        out = out * lax.rsqrt(jnp.mean(out * out, axis=-1, keepdims=True) + EPS)
        out = out * hn_ref[:, cols]
        o_ref[:, cols] = (out * _silu(g_ref[:, cols])).astype(o_ref.dtype)

        v_dec = (v * dec_ref[2, h]).astype(_BF16)
        state_ref[h] = state_ref[h] * cdec_ref[h] + _contract_rows(k, v_dec)


def _retention(zr, zf, logit_f, logit_b, head_gain, *, heads, head_dim, chunk=256):
    s = zr.shape[0]
    assert s % chunk == 0
    nc = s // chunk
    hd = head_dim
    rd = heads * hd
    decay_spec = pl.BlockSpec((heads, 1, 1), lambda c: (0, 0, 0))

    b_states = pl.pallas_call(
        functools.partial(_ret_state_kernel, chunk=chunk, heads=heads, hd=hd),
        grid=(nc,),
        in_specs=[
            decay_spec,
            pl.BlockSpec((chunk, rd), lambda t: (nc - 1 - t, 1)),
            pl.BlockSpec((chunk, rd), lambda t: (nc - 1 - t, 0)),
        ],
        out_specs=pl.BlockSpec((heads, 1, hd, hd), lambda t: (0, nc - 1 - t, 0, 0)),
        out_shape=jax.ShapeDtypeStruct((heads, nc, hd, hd), _BF16),
        scratch_shapes=[pltpu.VMEM((heads, hd, hd), _F32), pltpu.VMEM((heads, chunk, hd), _F32),
                        pltpu.VMEM((heads, 1, hd), _F32)],
        compiler_params=_params("arbitrary"),
        name="ret_state",
    )(logit_b, zr, zf)

    return pl.pallas_call(
        functools.partial(_ret_out_kernel, chunk=chunk, heads=heads, hd=hd),
        grid=(nc,),
        in_specs=[
            decay_spec,
            decay_spec,
            pl.BlockSpec((chunk, rd), lambda c: (c, 0)),
            pl.BlockSpec((chunk, rd), lambda c: (c, 1)),
            pl.BlockSpec((chunk, rd), lambda c: (c, 0)),
            pl.BlockSpec((chunk, rd), lambda c: (c, 1)),
            pl.BlockSpec((heads, 1, hd, hd), lambda c: (0, c, 0, 0)),
            pl.BlockSpec((1, rd), lambda c: (0, 0)),
        ],
        out_specs=pl.BlockSpec((chunk, rd), lambda c: (c, 0)),
        out_shape=jax.ShapeDtypeStruct((s, rd), _BF16),
        scratch_shapes=[pltpu.VMEM((heads, hd, hd), _F32), pltpu.VMEM((heads, chunk, chunk), _F32),
                        pltpu.VMEM((3, heads, chunk, hd), _F32), pltpu.VMEM((heads, 1, hd), _F32)],
        compiler_params=_params("arbitrary"),
        name="ret_out",
    )(logit_f, logit_b, zr, zr, zf, zf, b_states, head_gain)


def _window_sum(p, window):
    rows = p.shape[0]
    assert window >= 2 and window & (window - 1) == 0
    shifted = lambda a, k: pltpu.roll(a, k % rows, axis=0)
    s = shifted(p, 1) + p
    width = 2
    while width < window:
        s = shifted(s, width // 2) + shifted(s, -(width // 2))
        width *= 2
    return s


def _pool_kernel(prev_ref, cur_ref, next_ref, w_ref, scale_ref, o_ref, pad_ref, *, tm, seq, group_dim):
    i = pl.program_id(0)
    n_blocks = pl.num_programs(0)
    halo = POOL_HALO
    pad_ref[0:halo, :] = jnp.where(i > 0, prev_ref[...], 0.0)
    pad_ref[halo:halo + tm, :] = cur_ref[...]
    pad_ref[halo + tm:, :] = jnp.where(i < n_blocks - 1, next_ref[...], 0.0)

    row = i * tm + lax.broadcasted_iota(jnp.int32, (tm, 1), 0)
    for gi, window in enumerate(POOL_WINDOWS):
        lo = window // 2
        hi = window - 1 - lo
        cols = slice(gi * group_dim, (gi + 1) * group_dim)
        total = _window_sum(pad_ref[:, cols], window)[halo:halo + tm]
        count = (jnp.minimum(row + hi + 1, seq) - jnp.maximum(row - lo, 0)).astype(_F32)
        pooled = total / count - cur_ref[:, cols]
        mapped = jnp.dot(pooled.astype(_BF16), w_ref[gi], preferred_element_type=_F32)
        o_ref[:, cols] = (mapped * scale_ref[:, cols]).astype(o_ref.dtype)


def _pool(z, pool_w, pool_scale, *, tm=512):
    s, n = z.shape
    groups, group_dim, _ = pool_w.shape
    pool_dim = groups * group_dim
    assert groups == len(POOL_WINDOWS) and max(POOL_WINDOWS) // 2 <= POOL_HALO
    assert s % tm == 0 and tm % POOL_HALO == 0 and n % pool_dim == 0
    col = n // pool_dim - 1
    r = tm // POOL_HALO
    last = s // POOL_HALO - 1
    return pl.pallas_call(
        functools.partial(_pool_kernel, tm=tm, seq=s, group_dim=group_dim),
        grid=(s // tm,),
        in_specs=[
            pl.BlockSpec((POOL_HALO, pool_dim), lambda i: (jnp.maximum(i * r - 1, 0), col)),
            pl.BlockSpec((tm, pool_dim), lambda i: (i, col)),
            pl.BlockSpec((POOL_HALO, pool_dim), lambda i: (jnp.minimum((i + 1) * r, last), col)),
            pl.BlockSpec((groups, group_dim, group_dim), lambda i: (0, 0, 0)),
            pl.BlockSpec((1, pool_dim), lambda i: (0, 0)),
        ],
        out_specs=pl.BlockSpec((tm, pool_dim), lambda i: (i, 0)),
        out_shape=jax.ShapeDtypeStruct((s, pool_dim), _BF16),
        scratch_shapes=[pltpu.VMEM((tm + 2 * POOL_HALO, pool_dim), _F32)],
        compiler_params=_params("parallel"),
        name="pool",
    )(z, z, z, pool_w, pool_scale)


def _proj_out_kernel(x_ref, ret_ref, pool_ref, wr_ref, wp_ref, o_ref):
    acc = jnp.dot(ret_ref[...], wr_ref[...], preferred_element_type=_F32)
    acc += jnp.dot(pool_ref[...], wp_ref[...], preferred_element_type=_F32)
    o_ref[...] = x_ref[...] + acc


def _proj_out(x, ret, pool, w, *, tm=1024, tn=1024):
    s, d = x.shape
    kr = ret.shape[1]
    kp = pool.shape[1]
    assert kr == kp and w.shape == (kr + kp, d) and s % tm == 0 and d % tn == 0
    return pl.pallas_call(
        _proj_out_kernel,
        grid=(s // tm, d // tn),
        in_specs=[
            pl.BlockSpec((tm, tn), lambda i, j: (i, j)),
            pl.BlockSpec((tm, kr), lambda i, j: (i, 0)),
            pl.BlockSpec((tm, kp), lambda i, j: (i, 0)),
            pl.BlockSpec((kr, tn), lambda i, j: (0, j)),
            pl.BlockSpec((kp, tn), lambda i, j: (1, j)),
        ],
        out_specs=pl.BlockSpec((tm, tn), lambda i, j: (i, j)),
        out_shape=jax.ShapeDtypeStruct((s, d), _F32),
        compiler_params=_params("parallel", "arbitrary"),
        name="proj_out",
    )(x, ret, pool, w, w)


def kernel(x, positions, ffn1_norm, ffn1_w_gate, ffn1_w_up, ffn1_w_down, mix_norm, w_in,
           ret_decay_fwd, ret_decay_bwd, ret_head_norm, pool_w, pool_scale, w_out,
           ffn2_norm, ffn2_w_gate, ffn2_w_up, ffn2_w_down, final_norm):
    b, s, d = x.shape
    assert b == 1, "retention scan is written for a single sequence"
    depth = ffn1_norm.shape[0]
    heads = ret_decay_fwd.shape[1]
    ret_dim = ret_head_norm.shape[1]
    head_dim = ret_dim // heads

    freqs = 1.0 / (ROPE_BASE ** (jnp.arange(0, head_dim, 2, dtype=_F32) / head_dim))
    freqs = freqs.reshape(1, head_dim // 2)
    pos = positions.reshape(s, 1)
    final_gain = final_norm.reshape(1, d)
    row = lambda v: v.reshape(1, -1).astype(_F32)
    bf = lambda w: w.astype(_BF16)

    xs = x.reshape(s, d)
    for layer in range(depth):
        last = layer == depth - 1
        xs, w_in_bf, w_out_bf = _ffn(
            xs, row(ffn1_norm[layer]), ffn1_w_gate[layer], ffn1_w_up[layer], ffn1_w_down[layer],
            final_gain, final_norm=False, casts=(w_in[layer], w_out[layer]))
        zr, zf = _proj_in(xs, row(mix_norm[layer]), pos, freqs, w_in_bf,
                          ret_dim=ret_dim, head_dim=head_dim)
        ret = _retention(zr, zf, ret_decay_fwd[layer].reshape(heads, 1, 1).astype(_F32),
                         ret_decay_bwd[layer].reshape(heads, 1, 1).astype(_F32),
                         row(ret_head_norm[layer]), heads=heads, head_dim=head_dim)
        pool = _pool(zf, bf(pool_w[layer]), row(pool_scale[layer]))
        xs = _proj_out(xs, ret, pool, w_out_bf)
        xs, = _ffn(xs, row(ffn2_norm[layer]), ffn2_w_gate[layer], ffn2_w_up[layer], ffn2_w_down[layer],
                   final_gain, final_norm=last)
    return xs.reshape(b, s, d)
```

```python
import functools

import jax
import jax.numpy as jnp
from jax import lax
from jax.experimental import pallas as pl
from jax.experimental.pallas import tpu as pltpu

EPS = 1e-6
ROPE_BASE = 10000.0
POOL_WINDOWS = (2, 4, 8, 16)
POOL_HALO = 16

_F32 = jnp.float32
_BF16 = jnp.bfloat16
_VMEM_LIMIT_BYTES = 62 * 1024 * 1024


def _params(*semantics):
    return pltpu.CompilerParams(dimension_semantics=semantics, vmem_limit_bytes=_VMEM_LIMIT_BYTES)


def _rmsnorm(x, gain):
    return x * lax.rsqrt(jnp.mean(x * x, axis=-1, keepdims=True) + EPS) * gain


def _silu(g):
    return g * jax.nn.sigmoid(g)


def _ffn_head_kernel(x_hbm, gain_ref, wg_ref, wu_ref, wd_ref, o_hbm, wg_hbm, wu_hbm, wd_hbm,
                     acc_ref, h_ref, wg_bf, wu_bf, wd_bf, sem, w_sem,
                     *, n_steps, tf, row_chunk, row_part, col_chunk):
    j = pl.program_id(0)
    tm, d = acc_ref.shape

    def weight_copies(step):
        cols = pl.ds(pl.multiple_of(step * tf, tf), tf)
        return (pltpu.make_async_copy(wg_bf, wg_hbm.at[:, cols], w_sem.at[0]),
                pltpu.make_async_copy(wu_bf, wu_hbm.at[:, cols], w_sem.at[1]),
                pltpu.make_async_copy(wd_bf, wd_hbm.at[cols, :], w_sem.at[2]))

    @pl.when(j == 0)
    def _():
        copy = pltpu.make_async_copy(x_hbm.at[pl.ds(0, tm), :], acc_ref, sem)
        copy.start()
        copy.wait()
        for r in range(0, tm, row_chunk):
            rows = slice(r, r + row_chunk)
            h_ref[rows, :] = _rmsnorm(acc_ref[rows, :], gain_ref[...]).astype(_BF16)

    @pl.when(j > 0)
    def _():
        for copy in weight_copies(j - 1):
            copy.wait()

    wg_bf[...] = wg_ref[...].astype(_BF16)
    wu_bf[...] = wu_ref[...].astype(_BF16)
    wd_bf[...] = wd_ref[...].astype(_BF16)
    for copy in weight_copies(j):
        copy.start()

    parts = [slice(r, r + row_part) for r in range(0, tm, row_part)]
    acts = []
    for rows in parts:
        h = h_ref[rows, :]
        g = jnp.dot(h, wg_bf[...], preferred_element_type=_F32)
        u = jnp.dot(h, wu_bf[...], preferred_element_type=_F32)
        acts.append((_silu(g) * u * 0.5).astype(_BF16))
    for rows, a in zip(parts, acts):
        for c in range(0, d, col_chunk):
            cols = slice(c, c + col_chunk)
            acc_ref[rows, cols] += jnp.dot(a, wd_bf[:, cols], preferred_element_type=_F32)

    @pl.when(j == n_steps - 1)
    def _():
        for copy in weight_copies(j):
            copy.wait()
        copy = pltpu.make_async_copy(acc_ref, o_hbm, sem)
        copy.start()
        copy.wait()


def _ffn_head(x, gain, wg, wu, wd, *, rows, tf):
    s, d = x.shape
    f = wg.shape[1]
    assert f % tf == 0 and rows % 256 == 0
    n_steps = f // tf
    any_spec = pl.BlockSpec(memory_space=pl.ANY)
    return pl.pallas_call(
        functools.partial(_ffn_head_kernel, n_steps=n_steps, tf=tf, row_chunk=128, row_part=256,
                          col_chunk=1024),
        grid=(n_steps,),
        in_specs=[
            any_spec,
            pl.BlockSpec((1, d), lambda j: (0, 0)),
            pl.BlockSpec((d, tf), lambda j: (0, j)),
            pl.BlockSpec((d, tf), lambda j: (0, j)),
            pl.BlockSpec((tf, d), lambda j: (j, 0)),
        ],
        out_specs=[any_spec, any_spec, any_spec, any_spec],
        out_shape=[jax.ShapeDtypeStruct((rows, d), _F32), jax.ShapeDtypeStruct(wg.shape, _BF16),
                   jax.ShapeDtypeStruct(wu.shape, _BF16), jax.ShapeDtypeStruct(wd.shape, _BF16)],
        scratch_shapes=[pltpu.VMEM((rows, d), _F32), pltpu.VMEM((rows, d), _BF16),
                        pltpu.VMEM((d, tf), _BF16), pltpu.VMEM((d, tf), _BF16), pltpu.VMEM((tf, d), _BF16),
                        pltpu.SemaphoreType.DMA, pltpu.SemaphoreType.DMA((3,))],
        compiler_params=_params("arbitrary"),
        name="ffn_head",
    )(x, gain, wg, wu, wd)


def _ffn_kernel(x_hbm, gain_ref, wg_ref, wu_ref, wd_ref, fgain_ref, *rest, head_blocks,
                n_casts, n_steps, n_row_blocks, tf, tail_subs, final_norm, row_chunk, col_chunk):
    if head_blocks:
        head_hbm, rest = rest[0], rest[1:]
    cast_in = rest[:n_casts]
    o_ref = rest[n_casts]
    cast_out = rest[n_casts + 1:2 * n_casts + 1]
    x_buf, h_ref, x_sem = rest[2 * n_casts + 1:]
    i = pl.program_id(0)
    j = pl.program_id(1)
    tm, d = o_ref.shape
    computed = i >= head_blocks

    def x_copy(block):
        return pltpu.make_async_copy(x_hbm.at[pl.ds(block * tm, tm), :], x_buf, x_sem)

    def activation(k):
        cols = slice(k * tf, (k + 1) * tf)
        h = h_ref[...]
        g = jnp.dot(h, wg_ref[:, cols], preferred_element_type=_F32)
        u = jnp.dot(h, wu_ref[:, cols], preferred_element_type=_F32)
        return (_silu(g) * u * 0.5).astype(_BF16)

    def cast_blocks():
        for src, dst in zip(cast_in, cast_out):
            dst[...] = src[...].astype(dst.dtype)

    def step(n_subs):
        acts = [activation(k) for k in range(n_subs)]
        cast_blocks()
        for k, a in enumerate(acts):
            for c in range(0, d, col_chunk):
                cols = slice(c, c + col_chunk)
                o_ref[:, cols] += jnp.dot(a, wd_ref[k * tf:(k + 1) * tf, cols],
                                          preferred_element_type=_F32)

    if head_blocks:
        @pl.when(i < head_blocks)
        def _():
            @pl.when(j == 0)
            def _():
                copy = pltpu.make_async_copy(head_hbm.at[pl.ds(i * tm, tm), :], x_buf, x_sem)
                copy.start()
                copy.wait()
                for r in range(0, tm, row_chunk):
                    rows = slice(r, r + row_chunk)
                    y = x_buf[rows, :]
                    o_ref[rows, :] = _rmsnorm(y, fgain_ref[...]) if final_norm else y

            cast_blocks()

    @pl.when(jnp.logical_and(j == 0, computed))
    def _():
        if not head_blocks:
            @pl.when(i == 0)
            def _():
                x_copy(0).start()

        x_copy(i).wait()
        for r in range(0, tm, row_chunk):
            rows = slice(r, r + row_chunk)
            xr = x_buf[rows, :]
            h_ref[rows, :] = _rmsnorm(xr, gain_ref[...]).astype(_BF16)
            o_ref[rows, :] = xr
        step(2)

    @pl.when(j == 1)
    def _():
        @pl.when(jnp.logical_and(i + 1 < n_row_blocks, i + 1 >= head_blocks))
        def _():
            x_copy(i + 1).start()

    @pl.when(jnp.logical_and(jnp.logical_and(j > 0, j < n_steps - 1), computed))
    def _():
        step(2)

    @pl.when(jnp.logical_and(j == n_steps - 1, computed))
    def _():
        step(tail_subs)
        if final_norm:
            for r in range(0, tm, row_chunk):
                rows = slice(r, r + row_chunk)
                o_ref[rows, :] = _rmsnorm(o_ref[rows, :], fgain_ref[...])


_CAST_ROWS = 16


def _cast_specs(w, n_steps, total_steps):
    rows, width = w.shape
    assert rows % _CAST_ROWS == 0
    splits = [k for k in (4, 2, 1)
              if width % (k * 128) == 0 and (rows // _CAST_ROWS) * k <= total_steps]
    if not splits:
        return None
    k = splits[0]
    n_blocks = (rows // _CAST_ROWS) * k

    def index(i, j):
        b = jnp.minimum(i * n_steps + j, n_blocks - 1)
        return b // k, b % k

    block = (_CAST_ROWS, width // k)
    return pl.BlockSpec(block, index), pl.BlockSpec(block, index), jax.ShapeDtypeStruct(w.shape, _BF16)


def _ffn(x, gain, wg, wu, wd, fgain, *, final_norm, casts=(), tm=512, tf=256):
    s, d = x.shape
    f = wg.shape[1]
    assert s % tm == 0 and f % tf == 0
    n_steps = pl.cdiv(f, 2 * tf)
    assert n_steps >= 3
    tail_subs = (f - (n_steps - 1) * 2 * tf) // tf
    n_row_blocks = s // tm
    head_blocks = min(2, n_row_blocks - 1) if wg.dtype != _BF16 else 0
    head = ()
    if head_blocks:
        head_rows, wg, wu, wd = _ffn_head(x, gain, wg, wu, wd, rows=head_blocks * tm, tf=tf)
        head = (head_rows,)
    w_col = lambda i, j: jnp.where(i < head_blocks, 0, j)
    all_specs = [_cast_specs(w, n_steps, n_row_blocks * n_steps) for w in casts]
    hosted = [w for w, spec in zip(casts, all_specs) if spec is not None]
    cast_specs = [spec for spec in all_specs if spec is not None]
    kern = functools.partial(
        _ffn_kernel, head_blocks=head_blocks, n_casts=len(hosted), n_steps=n_steps, n_row_blocks=n_row_blocks,
        tf=tf, tail_subs=tail_subs, final_norm=final_norm, row_chunk=128, col_chunk=1024)
    out, *hosted_bf = pl.pallas_call(
        kern,
        grid=(n_row_blocks, n_steps),
        in_specs=[
            pl.BlockSpec(memory_space=pl.ANY),
            pl.BlockSpec((1, d), lambda i, j: (0, 0)),
            pl.BlockSpec((d, 2 * tf), lambda i, j: (0, w_col(i, j))),
            pl.BlockSpec((d, 2 * tf), lambda i, j: (0, w_col(i, j))),
            pl.BlockSpec((2 * tf, d), lambda i, j: (w_col(i, j), 0)),
            pl.BlockSpec((1, d), lambda i, j: (0, 0)),
        ] + [pl.BlockSpec(memory_space=pl.ANY)] * len(head) + [spec[0] for spec in cast_specs],
        out_specs=[pl.BlockSpec((tm, d), lambda i, j: (i, 0))] + [spec[1] for spec in cast_specs],
        out_shape=[jax.ShapeDtypeStruct((s, d), _F32)] + [spec[2] for spec in cast_specs],
        scratch_shapes=[pltpu.VMEM((tm, d), _F32), pltpu.VMEM((tm, d), _BF16),
                        pltpu.SemaphoreType.DMA],
        compiler_params=_params("arbitrary", "arbitrary"),
        name="ffn",
    )(x, gain, wg.astype(_BF16), wu.astype(_BF16), wd.astype(_BF16), fgain, *head, *hosted)
    hosted_bf = iter(hosted_bf)
    return [out] + [next(hosted_bf) if spec is not None else w.astype(_BF16)
                    for w, spec in zip(casts, all_specs)]


def _proj_in_kernel(x_hbm, gain_ref, pos_ref, freq_ref, w_ref, zr_ref, zf_ref,
                    x_buf, h_ref, cos_ref, sin_ref, x_sem,
                    *, n_row_blocks, n_plain, n_q_blocks, heads_per_block, head_dim, k_scale, row_chunk):
    i = pl.program_id(0)
    j = pl.program_id(1)
    tm = x_buf.shape[0]
    half = head_dim // 2

    def x_copy(block):
        return pltpu.make_async_copy(x_hbm.at[pl.ds(block * tm, tm), :], x_buf, x_sem)

    def head_dot(t):
        cols = slice(t * head_dim, (t + 1) * head_dim)
        return jnp.dot(h_ref[...], w_ref[:, cols], preferred_element_type=_F32)

    def plain_step():
        for t in range(heads_per_block):
            zf_ref[:, t * head_dim:(t + 1) * head_dim] = head_dot(t)

    @pl.when(j == 0)
    def _():
        @pl.when(i == 0)
        def _():
            x_copy(0).start()

        x_copy(i).wait()
        for r in range(0, tm, row_chunk):
            rows = slice(r, r + row_chunk)
            h_ref[rows, :] = _rmsnorm(x_buf[rows, :], gain_ref[...]).astype(_BF16)
        ang = pos_ref[...].astype(_F32) * freq_ref[...]
        cos_ref[...] = jnp.cos(ang)
        sin_ref[...] = jnp.sin(ang)
        plain_step()

    @pl.when(j == 1)
    def _():
        @pl.when(i + 1 < n_row_blocks)
        def _():
            x_copy(i + 1).start()

    @pl.when(jnp.logical_and(j > 0, j < n_plain))
    def _():
        plain_step()

    @pl.when(j >= n_plain)
    def _():
        cos = cos_ref[...]
        sin = sin_ref[...]
        scale = jnp.where(j - n_plain >= n_q_blocks, k_scale, 1.0).astype(_F32)
        for t in range(heads_per_block):
            z = head_dot(t)
            t1 = z[:, :half]
            t2 = z[:, half:]
            lo = t * head_dim
            zr_ref[:, lo:lo + half] = ((t1 * cos - t2 * sin) * scale).astype(zr_ref.dtype)
            zr_ref[:, lo + half:lo + head_dim] = ((t1 * sin + t2 * cos) * scale).astype(zr_ref.dtype)


def _proj_in(x, gain, pos, freqs, w, *, ret_dim, head_dim, tm=1024, tn=1024):
    s, d = x.shape
    n = w.shape[1]
    assert s % tm == 0 and n % tn == 0 and ret_dim % tn == 0 and tn % head_dim == 0
    half = head_dim // 2
    n_rot = 2 * ret_dim // tn
    n_plain = n // tn - n_rot
    assert n_plain >= 2
    kern = functools.partial(
        _proj_in_kernel, n_row_blocks=s // tm, n_plain=n_plain, n_q_blocks=ret_dim // tn,
        heads_per_block=tn // head_dim, head_dim=head_dim, k_scale=head_dim ** -0.5, row_chunk=128)
    return pl.pallas_call(
        kern,
        grid=(s // tm, n // tn),
        in_specs=[
            pl.BlockSpec(memory_space=pl.ANY),
            pl.BlockSpec((1, d), lambda i, j: (0, 0)),
            pl.BlockSpec((tm, 1), lambda i, j: (i, 0)),
            pl.BlockSpec((1, half), lambda i, j: (0, 0)),
            pl.BlockSpec((d, tn), lambda i, j: (0, jnp.where(j < n_plain, j + n_rot, j - n_plain))),
        ],
        out_specs=[
            pl.BlockSpec((tm, tn), lambda i, j: (i, jnp.maximum(j - n_plain, 0))),
            pl.BlockSpec((tm, tn), lambda i, j: (i, jnp.minimum(j, n_plain - 1))),
        ],
        out_shape=[jax.ShapeDtypeStruct((s, n_rot * tn), _BF16),
                   jax.ShapeDtypeStruct((s, n_plain * tn), _F32)],
        scratch_shapes=[pltpu.VMEM((tm, d), _F32), pltpu.VMEM((tm, d), _BF16),
                        pltpu.VMEM((tm, half), _F32), pltpu.VMEM((tm, half), _F32),
                        pltpu.SemaphoreType.DMA],
        compiler_params=_params("arbitrary", "arbitrary"),
        name="proj_in",
    )(x, gain, pos, freqs, w)


def _contract_rows(a, b):
    return lax.dot_general(a, b, (((0,), (0,)), ((), ())), preferred_element_type=_F32)


def _ret_state_kernel(lgb_ref, k_ref, v_ref, b_ref, state_ref, vdec_ref, cdec_ref, *, chunk, heads, hd):
    t = pl.program_id(0)

    @pl.when(t == 0)
    def _():
        state_ref[...] = jnp.zeros_like(state_ref)
        idx = lax.broadcasted_iota(jnp.int32, (chunk, hd), 0).astype(_F32)
        for h in range(heads):
            lgb = jax.nn.log_sigmoid(lgb_ref[h])
            vdec_ref[h] = jnp.exp(idx * lgb)
            cdec_ref[h] = jnp.broadcast_to(jnp.exp(chunk * lgb), (1, hd))

    for h in range(heads):
        cols = slice(h * hd, (h + 1) * hd)
        b_ref[h, 0] = state_ref[h].astype(_BF16)
        v_dec = (v_ref[:, cols] * vdec_ref[h]).astype(_BF16)
        state_ref[h] = state_ref[h] * cdec_ref[h] + _contract_rows(k_ref[:, cols], v_dec)


def _ret_out_kernel(lgf_ref, lgb_ref, qk_ref, vg_ref, b_ref, hn_ref, o_ref,
                    state_ref, dm_ref, dec_ref, cdec_ref, *, chunk, heads, hd):
    c = pl.program_id(0)
    rd = heads * hd

    @pl.when(c == 0)
    def _():
        state_ref[...] = jnp.zeros_like(state_ref)
        rel = (lax.broadcasted_iota(jnp.int32, (chunk, chunk), 0)
               - lax.broadcasted_iota(jnp.int32, (chunk, chunk), 1)).astype(_F32)
        idx = lax.broadcasted_iota(jnp.int32, (chunk, hd), 0).astype(_F32)
        for h in range(heads):
            lgf = jax.nn.log_sigmoid(lgf_ref[h])
            lgb = jax.nn.log_sigmoid(lgb_ref[h])
            dm_ref[h] = jnp.exp(jnp.where(rel >= 0, rel * lgf, -rel * lgb))
            dec_ref[0, h] = jnp.exp((idx + 1.0) * lgf)
            dec_ref[1, h] = jnp.exp((chunk - idx) * lgb)
            dec_ref[2, h] = jnp.exp((chunk - 1.0 - idx) * lgf)
            cdec_ref[h] = jnp.broadcast_to(jnp.exp(chunk * lgf), (1, hd))

    for h in range(heads):
        cols = slice(h * hd, (h + 1) * hd)
        second_cols = slice(rd + h * hd, rd + (h + 1) * hd)
        q = qk_ref[:, cols]
        k = qk_ref[:, second_cols]
        v = vg_ref[:, cols]
        scores = lax.dot_general(q, k, (((1,), (1,)), ((), ())), preferred_element_type=_F32)
        scores = scores * dm_ref[h]
        out = jnp.dot(scores.astype(_BF16), v.astype(_BF16), preferred_element_type=_F32)
        out += jnp.dot(q, state_ref[h].astype(_BF16), preferred_element_type=_F32) * dec_ref[0, h]
        out += jnp.dot(q, b_ref[h, 0], preferred_element_type=_F32) * dec_ref[1, h]

        out = out * lax.rsqrt(jnp.mean(out * out, axis=-1, keepdims=True) + EPS)
        out = out * hn_ref[:, cols]
        o_ref[:, cols] = (out * _silu(vg_ref[:, second_cols])).astype(o_ref.dtype)

        v_dec = (v * dec_ref[2, h]).astype(_BF16)
        state_ref[h] = state_ref[h] * cdec_ref[h] + _contract_rows(k, v_dec)


def _retention(zr, zf, logit_f, logit_b, head_gain, *, heads, head_dim, chunk=256):
    s = zr.shape[0]
    assert s % chunk == 0
    nc = s // chunk
    hd = head_dim
    rd = heads * hd
    assert zr.shape[1] == 2 * rd and zf.shape[1] >= 2 * rd
    decay_spec = pl.BlockSpec((heads, 1, 1), lambda c: (0, 0, 0))

    b_states = pl.pallas_call(
        functools.partial(_ret_state_kernel, chunk=chunk, heads=heads, hd=hd),
        grid=(nc,),
        in_specs=[
            decay_spec,
            pl.BlockSpec((chunk, rd), lambda t: (nc - 1 - t, 1)),
            pl.BlockSpec((chunk, rd), lambda t: (nc - 1 - t, 0)),
        ],
        out_specs=pl.BlockSpec((heads, 1, hd, hd), lambda t: (0, nc - 1 - t, 0, 0)),
        out_shape=jax.ShapeDtypeStruct((heads, nc, hd, hd), _BF16),
        scratch_shapes=[pltpu.VMEM((heads, hd, hd), _F32), pltpu.VMEM((heads, chunk, hd), _F32),
                        pltpu.VMEM((heads, 1, hd), _F32)],
        compiler_params=_params("arbitrary"),
        name="ret_state",
    )(logit_b, zr, zf)

    return pl.pallas_call(
        functools.partial(_ret_out_kernel, chunk=chunk, heads=heads, hd=hd),
        grid=(nc,),
        in_specs=[
            decay_spec,
            decay_spec,
            pl.BlockSpec((chunk, 2 * rd), lambda c: (c, 0)),
            pl.BlockSpec((chunk, 2 * rd), lambda c: (c, 0)),
            pl.BlockSpec((heads, 1, hd, hd), lambda c: (0, c, 0, 0)),
            pl.BlockSpec((1, rd), lambda c: (0, 0)),
        ],
        out_specs=pl.BlockSpec((chunk, rd), lambda c: (c, 0)),
        out_shape=jax.ShapeDtypeStruct((s, rd), _BF16),
        scratch_shapes=[pltpu.VMEM((heads, hd, hd), _F32), pltpu.VMEM((heads, chunk, chunk), _F32),
                        pltpu.VMEM((3, heads, chunk, hd), _F32), pltpu.VMEM((heads, 1, hd), _F32)],
        compiler_params=_params("arbitrary"),
        name="ret_out",
    )(logit_f, logit_b, zr, zf, b_states, head_gain)


def _window_sum(p, window):
    rows = p.shape[0]
    assert window >= 2 and window & (window - 1) == 0
    shifted = lambda a, k: pltpu.roll(a, k % rows, axis=0)
    s = shifted(p, 1) + p
    width = 2
    while width < window:
        s = shifted(s, width // 2) + shifted(s, -(width // 2))
        width *= 2
    return s


def _pool_kernel(prev_ref, cur_ref, next_ref, w_ref, scale_ref, o_ref, pad_ref, *, tm, seq, group_dim):
    i = pl.program_id(0)
    n_blocks = pl.num_programs(0)
    halo = POOL_HALO
    pad_ref[0:halo, :] = jnp.where(i > 0, prev_ref[...], 0.0)
    pad_ref[halo:halo + tm, :] = cur_ref[...]
    pad_ref[halo + tm:, :] = jnp.where(i < n_blocks - 1, next_ref[...], 0.0)

    row = i * tm + lax.broadcasted_iota(jnp.int32, (tm, 1), 0)
    for gi, window in enumerate(POOL_WINDOWS):
        lo = window // 2
        hi = window - 1 - lo
        cols = slice(gi * group_dim, (gi + 1) * group_dim)
        total = _window_sum(pad_ref[:, cols], window)[halo:halo + tm]
        count = (jnp.minimum(row + hi + 1, seq) - jnp.maximum(row - lo, 0)).astype(_F32)
        pooled = total / count - cur_ref[:, cols]
        mapped = jnp.dot(pooled.astype(_BF16), w_ref[gi], preferred_element_type=_F32)
        o_ref[:, cols] = (mapped * scale_ref[:, cols]).astype(o_ref.dtype)


def _pool(z, pool_w, pool_scale, *, tm=1024):
    s, n = z.shape
    groups, group_dim, _ = pool_w.shape
    pool_dim = groups * group_dim
    assert groups == len(POOL_WINDOWS) and max(POOL_WINDOWS) // 2 <= POOL_HALO
    assert s % tm == 0 and tm % POOL_HALO == 0 and n % pool_dim == 0
    col = n // pool_dim - 1
    r = tm // POOL_HALO
    last = s // POOL_HALO - 1
    return pl.pallas_call(
        functools.partial(_pool_kernel, tm=tm, seq=s, group_dim=group_dim),
        grid=(s // tm,),
        in_specs=[
            pl.BlockSpec((POOL_HALO, pool_dim), lambda i: (jnp.maximum(i * r - 1, 0), col)),
            pl.BlockSpec((tm, pool_dim), lambda i: (i, col)),
            pl.BlockSpec((POOL_HALO, pool_dim), lambda i: (jnp.minimum((i + 1) * r, last), col)),
            pl.BlockSpec((groups, group_dim, group_dim), lambda i: (0, 0, 0)),
            pl.BlockSpec((1, pool_dim), lambda i: (0, 0)),
        ],
        out_specs=pl.BlockSpec((tm, pool_dim), lambda i: (i, 0)),
        out_shape=jax.ShapeDtypeStruct((s, pool_dim), _BF16),
        scratch_shapes=[pltpu.VMEM((tm + 2 * POOL_HALO, pool_dim), _F32)],
        compiler_params=_params("parallel"),
        name="pool",
    )(z, z, z, pool_w, pool_scale)


def _proj_out_kernel(x_ref, ret_ref, pool_ref, wr_ref, wp_ref, o_ref):
    acc = jnp.dot(ret_ref[...], wr_ref[...], preferred_element_type=_F32)
    acc += jnp.dot(pool_ref[...], wp_ref[...], preferred_element_type=_F32)
    o_ref[...] = x_ref[...] + acc


def _proj_out(x, ret, pool, w, *, tm=1024, tn=1024):
    s, d = x.shape
    kr = ret.shape[1]
    kp = pool.shape[1]
    assert kr == kp and w.shape == (kr + kp, d) and s % tm == 0 and d % tn == 0
    return pl.pallas_call(
        _proj_out_kernel,
        grid=(s // tm, d // tn),
        in_specs=[
            pl.BlockSpec((tm, tn), lambda i, j: (i, j)),
            pl.BlockSpec((tm, kr), lambda i, j: (i, 0)),
            pl.BlockSpec((tm, kp), lambda i, j: (i, 0)),
            pl.BlockSpec((kr, tn), lambda i, j: (0, j)),
            pl.BlockSpec((kp, tn), lambda i, j: (1, j)),
        ],
        out_specs=pl.BlockSpec((tm, tn), lambda i, j: (i, j)),
        out_shape=jax.ShapeDtypeStruct((s, d), _F32),
        compiler_params=_params("parallel", "arbitrary"),
        name="proj_out",
    )(x, ret, pool, w, w)


def kernel(x, positions, ffn1_norm, ffn1_w_gate, ffn1_w_up, ffn1_w_down, mix_norm, w_in,
           ret_decay_fwd, ret_decay_bwd, ret_head_norm, pool_w, pool_scale, w_out,
           ffn2_norm, ffn2_w_gate, ffn2_w_up, ffn2_w_down, final_norm):
    b, s, d = x.shape
    assert b == 1, "retention scan is written for a single sequence"
    depth = ffn1_norm.shape[0]
    heads = ret_decay_fwd.shape[1]
    ret_dim = ret_head_norm.shape[1]
    head_dim = ret_dim // heads

    freqs = 1.0 / (ROPE_BASE ** (jnp.arange(0, head_dim, 2, dtype=_F32) / head_dim))
    freqs = freqs.reshape(1, head_dim // 2)
    pos = positions.reshape(s, 1)
    final_gain = final_norm.reshape(1, d)
    row = lambda v: v.reshape(1, -1).astype(_F32)
    bf = lambda w: w.astype(_BF16)

    xs = x.reshape(s, d)
    for layer in range(depth):
        last = layer == depth - 1
        xs, w_in_bf, w_out_bf = _ffn(
            xs, row(ffn1_norm[layer]), ffn1_w_gate[layer], ffn1_w_up[layer], ffn1_w_down[layer],
            final_gain, final_norm=False, casts=(w_in[layer], w_out[layer]))
        zr, zf = _proj_in(xs, row(mix_norm[layer]), pos, freqs, w_in_bf,
                          ret_dim=ret_dim, head_dim=head_dim)
        ret = _retention(zr, zf, ret_decay_fwd[layer].reshape(heads, 1, 1).astype(_F32),
                         ret_decay_bwd[layer].reshape(heads, 1, 1).astype(_F32),
                         row(ret_head_norm[layer]), heads=heads, head_dim=head_dim)
        pool = _pool(zf, bf(pool_w[layer]), row(pool_scale[layer]))
        xs = _proj_out(xs, ret, pool, w_out_bf)
        xs, = _ffn(xs, row(ffn2_norm[layer]), ffn2_w_gate[layer], ffn2_w_up[layer], ffn2_w_down[layer],
                   final_gain, final_norm=last)
    return xs.reshape(b, s, d)
```

```python
import functools

import jax
import jax.numpy as jnp
from jax import lax
from jax.experimental import pallas as pl
from jax.experimental.pallas import tpu as pltpu

EPS = 1e-6
ROPE_BASE = 10000.0
POOL_WINDOWS = (2, 4, 8, 16)
POOL_HALO = 16

_F32 = jnp.float32
_BF16 = jnp.bfloat16
_VMEM_LIMIT_BYTES = 62 * 1024 * 1024


def _params(*semantics):
    return pltpu.CompilerParams(dimension_semantics=semantics, vmem_limit_bytes=_VMEM_LIMIT_BYTES)


def _rmsnorm(x, gain):
    return x * lax.rsqrt(jnp.mean(x * x, axis=-1, keepdims=True) + EPS) * gain


def _silu(g):
    return g * jax.nn.sigmoid(g)


def _ffn_head_kernel(x_hbm, gain_ref, wg_ref, wu_ref, wd_ref, o_hbm, wg_hbm, wu_hbm, wd_hbm,
                     acc_ref, h_ref, wg_bf, wu_bf, wd_bf, sem, w_sem,
                     *, n_steps, tf, row_chunk, row_part, col_chunk):
    j = pl.program_id(0)
    tm, d = acc_ref.shape

    def weight_copies(step):
        cols = pl.ds(pl.multiple_of(step * tf, tf), tf)
        return (pltpu.make_async_copy(wg_bf, wg_hbm.at[:, cols], w_sem.at[0]),
                pltpu.make_async_copy(wu_bf, wu_hbm.at[:, cols], w_sem.at[1]),
                pltpu.make_async_copy(wd_bf, wd_hbm.at[cols, :], w_sem.at[2]))

    @pl.when(j == 0)
    def _():
        copy = pltpu.make_async_copy(x_hbm.at[pl.ds(0, tm), :], acc_ref, sem)
        copy.start()
        copy.wait()
        for r in range(0, tm, row_chunk):
            rows = slice(r, r + row_chunk)
            h_ref[rows, :] = _rmsnorm(acc_ref[rows, :], gain_ref[...]).astype(_BF16)

    @pl.when(j > 0)
    def _():
        for copy in weight_copies(j - 1):
            copy.wait()

    wg_bf[...] = wg_ref[...].astype(_BF16)
    wu_bf[...] = wu_ref[...].astype(_BF16)
    wd_bf[...] = wd_ref[...].astype(_BF16)
    for copy in weight_copies(j):
        copy.start()

    parts = [slice(r, r + row_part) for r in range(0, tm, row_part)]
    acts = []
    for rows in parts:
        h = h_ref[rows, :]
        g = jnp.dot(h, wg_bf[...], preferred_element_type=_F32)
        u = jnp.dot(h, wu_bf[...], preferred_element_type=_F32)
        acts.append((_silu(g) * u * 0.5).astype(_BF16))
    for rows, a in zip(parts, acts):
        for c in range(0, d, col_chunk):
            cols = slice(c, c + col_chunk)
            acc_ref[rows, cols] += jnp.dot(a, wd_bf[:, cols], preferred_element_type=_F32)

    @pl.when(j == n_steps - 1)
    def _():
        for copy in weight_copies(j):
            copy.wait()
        copy = pltpu.make_async_copy(acc_ref, o_hbm, sem)
        copy.start()
        copy.wait()


def _ffn_head(x, gain, wg, wu, wd, *, rows, tf):
    s, d = x.shape
    f = wg.shape[1]
    assert f % tf == 0 and rows % 256 == 0
    n_steps = f // tf
    any_spec = pl.BlockSpec(memory_space=pl.ANY)
    return pl.pallas_call(
        functools.partial(_ffn_head_kernel, n_steps=n_steps, tf=tf, row_chunk=128, row_part=256,
                          col_chunk=1024),
        grid=(n_steps,),
        in_specs=[
            any_spec,
            pl.BlockSpec((1, d), lambda j: (0, 0)),
            pl.BlockSpec((d, tf), lambda j: (0, j)),
            pl.BlockSpec((d, tf), lambda j: (0, j)),
            pl.BlockSpec((tf, d), lambda j: (j, 0)),
        ],
        out_specs=[any_spec, any_spec, any_spec, any_spec],
        out_shape=[jax.ShapeDtypeStruct((rows, d), _F32), jax.ShapeDtypeStruct(wg.shape, _BF16),
                   jax.ShapeDtypeStruct(wu.shape, _BF16), jax.ShapeDtypeStruct(wd.shape, _BF16)],
        scratch_shapes=[pltpu.VMEM((rows, d), _F32), pltpu.VMEM((rows, d), _BF16),
                        pltpu.VMEM((d, tf), _BF16), pltpu.VMEM((d, tf), _BF16), pltpu.VMEM((tf, d), _BF16),
                        pltpu.SemaphoreType.DMA, pltpu.SemaphoreType.DMA((3,))],
        compiler_params=_params("arbitrary"),
        name="ffn_head",
    )(x, gain, wg, wu, wd)


def _ffn_kernel(x_hbm, gain_ref, wg_ref, wu_ref, wd_ref, fgain_ref, *rest, head_blocks,
                n_casts, n_steps, n_row_blocks, tf, tail_subs, final_norm, row_chunk, col_chunk):
    if head_blocks:
        head_hbm, rest = rest[0], rest[1:]
    cast_in = rest[:n_casts]
    o_ref = rest[n_casts]
    cast_out = rest[n_casts + 1:2 * n_casts + 1]
    x_buf, h_ref, x_sem = rest[2 * n_casts + 1:]
    i = pl.program_id(0)
    j = pl.program_id(1)
    tm, d = o_ref.shape
    computed = i >= head_blocks

    def x_copy(block):
        return pltpu.make_async_copy(x_hbm.at[pl.ds(block * tm, tm), :], x_buf, x_sem)

    def activation(k):
        cols = slice(k * tf, (k + 1) * tf)
        h = h_ref[...]
        g = jnp.dot(h, wg_ref[:, cols], preferred_element_type=_F32)
        u = jnp.dot(h, wu_ref[:, cols], preferred_element_type=_F32)
        return (_silu(g) * u * 0.5).astype(_BF16)

    def cast_blocks():
        for src, dst in zip(cast_in, cast_out):
            dst[...] = src[...].astype(dst.dtype)

    def step(n_subs):
        acts = [activation(k) for k in range(n_subs)]
        cast_blocks()
        for k, a in enumerate(acts):
            for c in range(0, d, col_chunk):
                cols = slice(c, c + col_chunk)
                o_ref[:, cols] += jnp.dot(a, wd_ref[k * tf:(k + 1) * tf, cols],
                                          preferred_element_type=_F32)

    if head_blocks:
        @pl.when(i < head_blocks)
        def _():
            @pl.when(j == 0)
            def _():
                copy = pltpu.make_async_copy(head_hbm.at[pl.ds(i * tm, tm), :], x_buf, x_sem)
                copy.start()
                copy.wait()
                for r in range(0, tm, row_chunk):
                    rows = slice(r, r + row_chunk)
                    y = x_buf[rows, :]
                    o_ref[rows, :] = _rmsnorm(y, fgain_ref[...]) if final_norm else y

            cast_blocks()

    @pl.when(jnp.logical_and(j == 0, computed))
    def _():
        if not head_blocks:
            @pl.when(i == 0)
            def _():
                x_copy(0).start()

        x_copy(i).wait()
        for r in range(0, tm, row_chunk):
            rows = slice(r, r + row_chunk)
            xr = x_buf[rows, :]
            h_ref[rows, :] = _rmsnorm(xr, gain_ref[...]).astype(_BF16)
            o_ref[rows, :] = xr
        step(2)

    @pl.when(j == 1)
    def _():
        @pl.when(jnp.logical_and(i + 1 < n_row_blocks, i + 1 >= head_blocks))
        def _():
            x_copy(i + 1).start()

    @pl.when(jnp.logical_and(jnp.logical_and(j > 0, j < n_steps - 1), computed))
    def _():
        step(2)

    @pl.when(jnp.logical_and(j == n_steps - 1, computed))
    def _():
        step(tail_subs)
        if final_norm:
            for r in range(0, tm, row_chunk):
                rows = slice(r, r + row_chunk)
                o_ref[rows, :] = _rmsnorm(o_ref[rows, :], fgain_ref[...])


_CAST_ROWS = 16


def _cast_specs(w, n_steps, total_steps):
    rows, width = w.shape
    assert rows % _CAST_ROWS == 0
    splits = [k for k in (4, 2, 1)
              if width % (k * 128) == 0 and (rows // _CAST_ROWS) * k <= total_steps]
    if not splits:
        return None
    k = splits[0]
    n_blocks = (rows // _CAST_ROWS) * k

    def index(i, j):
        b = jnp.minimum(i * n_steps + j, n_blocks - 1)
        return b // k, b % k

    block = (_CAST_ROWS, width // k)
    return pl.BlockSpec(block, index), pl.BlockSpec(block, index), jax.ShapeDtypeStruct(w.shape, _BF16)


def _ffn(x, gain, wg, wu, wd, fgain, *, final_norm, casts=(), tm=512, tf=256):
    s, d = x.shape
    f = wg.shape[1]
    assert s % tm == 0 and f % tf == 0
    n_steps = pl.cdiv(f, 2 * tf)
    assert n_steps >= 3
    tail_subs = (f - (n_steps - 1) * 2 * tf) // tf
    n_row_blocks = s // tm
    head_blocks = min(2, n_row_blocks - 1) if wg.dtype != _BF16 else 0
    head = ()
    if head_blocks:
        head_rows, wg, wu, wd = _ffn_head(x, gain, wg, wu, wd, rows=head_blocks * tm, tf=tf)
        head = (head_rows,)
    w_col = lambda i, j: jnp.where(i < head_blocks, 0, j)
    all_specs = [_cast_specs(w, n_steps, n_row_blocks * n_steps) for w in casts]
    hosted = [w for w, spec in zip(casts, all_specs) if spec is not None]
    cast_specs = [spec for spec in all_specs if spec is not None]
    kern = functools.partial(
        _ffn_kernel, head_blocks=head_blocks, n_casts=len(hosted), n_steps=n_steps, n_row_blocks=n_row_blocks,
        tf=tf, tail_subs=tail_subs, final_norm=final_norm, row_chunk=128, col_chunk=1024)
    out, *hosted_bf = pl.pallas_call(
        kern,
        grid=(n_row_blocks, n_steps),
        in_specs=[
            pl.BlockSpec(memory_space=pl.ANY),
            pl.BlockSpec((1, d), lambda i, j: (0, 0)),
            pl.BlockSpec((d, 2 * tf), lambda i, j: (0, w_col(i, j))),
            pl.BlockSpec((d, 2 * tf), lambda i, j: (0, w_col(i, j))),
            pl.BlockSpec((2 * tf, d), lambda i, j: (w_col(i, j), 0)),
            pl.BlockSpec((1, d), lambda i, j: (0, 0)),
        ] + [pl.BlockSpec(memory_space=pl.ANY)] * len(head) + [spec[0] for spec in cast_specs],
        out_specs=[pl.BlockSpec((tm, d), lambda i, j: (i, 0))] + [spec[1] for spec in cast_specs],
        out_shape=[jax.ShapeDtypeStruct((s, d), _F32)] + [spec[2] for spec in cast_specs],
        scratch_shapes=[pltpu.VMEM((tm, d), _F32), pltpu.VMEM((tm, d), _BF16),
                        pltpu.SemaphoreType.DMA],
        compiler_params=_params("arbitrary", "arbitrary"),
        name="ffn",
    )(x, gain, wg.astype(_BF16), wu.astype(_BF16), wd.astype(_BF16), fgain, *head, *hosted)
    hosted_bf = iter(hosted_bf)
    return [out] + [next(hosted_bf) if spec is not None else w.astype(_BF16)
                    for w, spec in zip(casts, all_specs)]


def _proj_in_kernel(x_hbm, gain_ref, pos_ref, freq_ref, w_ref, zr_ref, zf_ref,
                    x_buf, h_ref, cos_ref, sin_ref, x_sem,
                    *, n_row_blocks, n_plain, n_q_blocks, heads_per_block, head_dim, k_scale, row_chunk):
    i = pl.program_id(0)
    j = pl.program_id(1)
    tm = x_buf.shape[0]
    half = head_dim // 2

    def x_copy(block):
        return pltpu.make_async_copy(x_hbm.at[pl.ds(block * tm, tm), :], x_buf, x_sem)

    def head_dot(t):
        cols = slice(t * head_dim, (t + 1) * head_dim)
        return jnp.dot(h_ref[...], w_ref[:, cols], preferred_element_type=_F32)

    def plain_step():
        for t in range(heads_per_block):
            zf_ref[:, t * head_dim:(t + 1) * head_dim] = head_dot(t)

    @pl.when(j == 0)
    def _():
        @pl.when(i == 0)
        def _():
            x_copy(0).start()

        x_copy(i).wait()
        for r in range(0, tm, row_chunk):
            rows = slice(r, r + row_chunk)
            h_ref[rows, :] = _rmsnorm(x_buf[rows, :], gain_ref[...]).astype(_BF16)
        ang = pos_ref[...].astype(_F32) * freq_ref[...]
        cos_ref[...] = jnp.cos(ang)
        sin_ref[...] = jnp.sin(ang)
        plain_step()

    @pl.when(j == 1)
    def _():
        @pl.when(i + 1 < n_row_blocks)
        def _():
            x_copy(i + 1).start()

    @pl.when(jnp.logical_and(j > 0, j < n_plain))
    def _():
        plain_step()

    @pl.when(j >= n_plain)
    def _():
        cos = cos_ref[...]
        sin = sin_ref[...]
        scale = jnp.where(j - n_plain >= n_q_blocks, k_scale, 1.0).astype(_F32)
        for t in range(heads_per_block):
            z = head_dot(t)
            t1 = z[:, :half]
            t2 = z[:, half:]
            lo = t * head_dim
            zr_ref[:, lo:lo + half] = ((t1 * cos - t2 * sin) * scale).astype(zr_ref.dtype)
            zr_ref[:, lo + half:lo + head_dim] = ((t1 * sin + t2 * cos) * scale).astype(zr_ref.dtype)


def _proj_in(x, gain, pos, freqs, w, *, ret_dim, head_dim, tm=1024, tn=1024):
    s, d = x.shape
    n = w.shape[1]
    assert s % tm == 0 and n % tn == 0 and ret_dim % tn == 0 and tn % head_dim == 0
    half = head_dim // 2
    n_rot = 2 * ret_dim // tn
    n_plain = n // tn - n_rot
    assert n_plain >= 2
    kern = functools.partial(
        _proj_in_kernel, n_row_blocks=s // tm, n_plain=n_plain, n_q_blocks=ret_dim // tn,
        heads_per_block=tn // head_dim, head_dim=head_dim, k_scale=head_dim ** -0.5, row_chunk=128)
    return pl.pallas_call(
        kern,
        grid=(s // tm, n // tn),
        in_specs=[
            pl.BlockSpec(memory_space=pl.ANY),
            pl.BlockSpec((1, d), lambda i, j: (0, 0)),
            pl.BlockSpec((tm, 1), lambda i, j: (i, 0)),
            pl.BlockSpec((1, half), lambda i, j: (0, 0)),
            pl.BlockSpec((d, tn), lambda i, j: (0, jnp.where(j < n_plain, j + n_rot, j - n_plain))),
        ],
        out_specs=[
            pl.BlockSpec((tm, tn), lambda i, j: (i, jnp.maximum(j - n_plain, 0))),
            pl.BlockSpec((tm, tn), lambda i, j: (i, jnp.minimum(j, n_plain - 1))),
        ],
        out_shape=[jax.ShapeDtypeStruct((s, n_rot * tn), _BF16),
                   jax.ShapeDtypeStruct((s, n_plain * tn), _F32)],
        scratch_shapes=[pltpu.VMEM((tm, d), _F32), pltpu.VMEM((tm, d), _BF16),
                        pltpu.VMEM((tm, half), _F32), pltpu.VMEM((tm, half), _F32),
                        pltpu.SemaphoreType.DMA],
        compiler_params=_params("arbitrary", "arbitrary"),
        name="proj_in",
    )(x, gain, pos, freqs, w)


def _contract_rows(a, b):
    return lax.dot_general(a, b, (((0,), (0,)), ((), ())), preferred_element_type=_F32)


def _ret_state_kernel(lgb_ref, k_ref, v_ref, b_ref, state_ref, vdec_ref, cdec_ref, *, chunk, heads, hd):
    t = pl.program_id(0)

    @pl.when(t == 0)
    def _():
        state_ref[...] = jnp.zeros_like(state_ref)
        idx = lax.broadcasted_iota(jnp.int32, (chunk, hd), 0).astype(_F32)
        for h in range(heads):
            lgb = jax.nn.log_sigmoid(lgb_ref[h])
            vdec_ref[h] = jnp.exp(idx * lgb)
            cdec_ref[h] = jnp.broadcast_to(jnp.exp(chunk * lgb), (1, hd))

    for h in range(heads):
        cols = slice(h * hd, (h + 1) * hd)
        b_ref[h, 0] = state_ref[h].astype(_BF16)
        v_dec = (v_ref[:, cols] * vdec_ref[h]).astype(_BF16)
        state_ref[h] = state_ref[h] * cdec_ref[h] + _contract_rows(k_ref[:, cols], v_dec)


def _ret_out_kernel(lgf_ref, lgb_ref, qk_ref, vg_ref, b_ref, hn_ref, o_ref,
                    state_ref, dm_ref, dec_ref, cdec_ref, *, chunk, heads, hd):
    c = pl.program_id(0)
    rd = heads * hd

    @pl.when(c == 0)
    def _():
        state_ref[...] = jnp.zeros_like(state_ref)
        rel = (lax.broadcasted_iota(jnp.int32, (chunk, chunk), 0)
               - lax.broadcasted_iota(jnp.int32, (chunk, chunk), 1)).astype(_F32)
        idx = lax.broadcasted_iota(jnp.int32, (chunk, hd), 0).astype(_F32)
        for h in range(heads):
            lgf = jax.nn.log_sigmoid(lgf_ref[h])
            lgb = jax.nn.log_sigmoid(lgb_ref[h])
            dm_ref[h] = jnp.exp(jnp.where(rel >= 0, rel * lgf, -rel * lgb))
            dec_ref[0, h] = jnp.exp((idx + 1.0) * lgf)
            dec_ref[1, h] = jnp.exp((chunk - idx) * lgb)
            dec_ref[2, h] = jnp.exp((chunk - 1.0 - idx) * lgf)
            cdec_ref[h] = jnp.broadcast_to(jnp.exp(chunk * lgf), (1, hd))

    for h in range(heads):
        cols = slice(h * hd, (h + 1) * hd)
        second_cols = slice(rd + h * hd, rd + (h + 1) * hd)
        q = qk_ref[:, cols]
        k = qk_ref[:, second_cols]
        v = vg_ref[:, cols]
        scores = lax.dot_general(q, k, (((1,), (1,)), ((), ())), preferred_element_type=_F32)
        scores = scores * dm_ref[h]
        out = jnp.dot(scores.astype(_BF16), v.astype(_BF16), preferred_element_type=_F32)
        out += jnp.dot(q, state_ref[h].astype(_BF16), preferred_element_type=_F32) * dec_ref[0, h]
        out += jnp.dot(q, b_ref[h, 0], preferred_element_type=_F32) * dec_ref[1, h]

        out = out * lax.rsqrt(jnp.mean(out * out, axis=-1, keepdims=True) + EPS)
        out = out * hn_ref[:, cols]
        o_ref[:, cols] = (out * _silu(vg_ref[:, second_cols])).astype(o_ref.dtype)

        v_dec = (v * dec_ref[2, h]).astype(_BF16)
        state_ref[h] = state_ref[h] * cdec_ref[h] + _contract_rows(k, v_dec)


def _retention(zr, zf, logit_f, logit_b, head_gain, *, heads, head_dim, chunk=512):
    s = zr.shape[0]
    assert s % chunk == 0
    nc = s // chunk
    hd = head_dim
    rd = heads * hd
    assert zr.shape[1] == 2 * rd and zf.shape[1] >= 2 * rd
    decay_spec = pl.BlockSpec((heads, 1, 1), lambda c: (0, 0, 0))

    b_states = pl.pallas_call(
        functools.partial(_ret_state_kernel, chunk=chunk, heads=heads, hd=hd),
        grid=(nc,),
        in_specs=[
            decay_spec,
            pl.BlockSpec((chunk, rd), lambda t: (nc - 1 - t, 1)),
            pl.BlockSpec((chunk, rd), lambda t: (nc - 1 - t, 0)),
        ],
        out_specs=pl.BlockSpec((heads, 1, hd, hd), lambda t: (0, nc - 1 - t, 0, 0)),
        out_shape=jax.ShapeDtypeStruct((heads, nc, hd, hd), _BF16),
        scratch_shapes=[pltpu.VMEM((heads, hd, hd), _F32), pltpu.VMEM((heads, chunk, hd), _F32),
                        pltpu.VMEM((heads, 1, hd), _F32)],
        compiler_params=_params("arbitrary"),
        name="ret_state",
    )(logit_b, zr, zf)

    return pl.pallas_call(
        functools.partial(_ret_out_kernel, chunk=chunk, heads=heads, hd=hd),
        grid=(nc,),
        in_specs=[
            decay_spec,
            decay_spec,
            pl.BlockSpec((chunk, 2 * rd), lambda c: (c, 0)),
            pl.BlockSpec((chunk, 2 * rd), lambda c: (c, 0)),
            pl.BlockSpec((heads, 1, hd, hd), lambda c: (0, c, 0, 0)),
            pl.BlockSpec((1, rd), lambda c: (0, 0)),
        ],
        out_specs=pl.BlockSpec((chunk, rd), lambda c: (c, 0)),
        out_shape=jax.ShapeDtypeStruct((s, rd), _BF16),
        scratch_shapes=[pltpu.VMEM((heads, hd, hd), _F32), pltpu.VMEM((heads, chunk, chunk), _F32),
                        pltpu.VMEM((3, heads, chunk, hd), _F32), pltpu.VMEM((heads, 1, hd), _F32)],
        compiler_params=_params("arbitrary"),
        name="ret_out",
    )(logit_f, logit_b, zr, zf, b_states, head_gain)


def _window_sum(p, window):
    rows = p.shape[0]
    assert window >= 2 and window & (window - 1) == 0
    shifted = lambda a, k: pltpu.roll(a, k % rows, axis=0)
    s = shifted(p, 1) + p
    width = 2
    while width < window:
        s = shifted(s, width // 2) + shifted(s, -(width // 2))
        width *= 2
    return s


def _pool_kernel(prev_ref, cur_ref, next_ref, w_ref, scale_ref, o_ref, pad_ref, *, tm, seq, group_dim):
    i = pl.program_id(0)
    n_blocks = pl.num_programs(0)
    halo = POOL_HALO
    pad_ref[0:halo, :] = jnp.where(i > 0, prev_ref[...], 0.0)
    pad_ref[halo:halo + tm, :] = cur_ref[...]
    pad_ref[halo + tm:, :] = jnp.where(i < n_blocks - 1, next_ref[...], 0.0)

    row = i * tm + lax.broadcasted_iota(jnp.int32, (tm, 1), 0)
    for gi, window in enumerate(POOL_WINDOWS):
        lo = window // 2
        hi = window - 1 - lo
        cols = slice(gi * group_dim, (gi + 1) * group_dim)
        total = _window_sum(pad_ref[:, cols], window)[halo:halo + tm]
        count = (jnp.minimum(row + hi + 1, seq) - jnp.maximum(row - lo, 0)).astype(_F32)
        pooled = total / count - cur_ref[:, cols]
        mapped = jnp.dot(pooled.astype(_BF16), w_ref[gi], preferred_element_type=_F32)
        o_ref[:, cols] = (mapped * scale_ref[:, cols]).astype(o_ref.dtype)


def _pool(z, pool_w, pool_scale, *, tm=1024):
    s, n = z.shape
    groups, group_dim, _ = pool_w.shape
    pool_dim = groups * group_dim
    assert groups == len(POOL_WINDOWS) and max(POOL_WINDOWS) // 2 <= POOL_HALO
    assert s % tm == 0 and tm % POOL_HALO == 0 and n % pool_dim == 0
    col = n // pool_dim - 1
    r = tm // POOL_HALO
    last = s // POOL_HALO - 1
    return pl.pallas_call(
        functools.partial(_pool_kernel, tm=tm, seq=s, group_dim=group_dim),
        grid=(s // tm,),
        in_specs=[
            pl.BlockSpec((POOL_HALO, pool_dim), lambda i: (jnp.maximum(i * r - 1, 0), col)),
            pl.BlockSpec((tm, pool_dim), lambda i: (i, col)),
            pl.BlockSpec((POOL_HALO, pool_dim), lambda i: (jnp.minimum((i + 1) * r, last), col)),
            pl.BlockSpec((groups, group_dim, group_dim), lambda i: (0, 0, 0)),
            pl.BlockSpec((1, pool_dim), lambda i: (0, 0)),
        ],
        out_specs=pl.BlockSpec((tm, pool_dim), lambda i: (i, 0)),
        out_shape=jax.ShapeDtypeStruct((s, pool_dim), _BF16),
        scratch_shapes=[pltpu.VMEM((tm + 2 * POOL_HALO, pool_dim), _F32)],
        compiler_params=_params("parallel"),
        name="pool",
    )(z, z, z, pool_w, pool_scale)


def _proj_out_kernel(x_ref, ret_ref, pool_ref, wr_ref, wp_ref, o_ref):
    acc = jnp.dot(ret_ref[...], wr_ref[...], preferred_element_type=_F32)
    acc += jnp.dot(pool_ref[...], wp_ref[...], preferred_element_type=_F32)
    o_ref[...] = x_ref[...] + acc


def _proj_out(x, ret, pool, w, *, tm=1024, tn=1024):
    s, d = x.shape
    kr = ret.shape[1]
    kp = pool.shape[1]
    assert kr == kp and w.shape == (kr + kp, d) and s % tm == 0 and d % tn == 0
    return pl.pallas_call(
        _proj_out_kernel,
        grid=(s // tm, d // tn),
        in_specs=[
            pl.BlockSpec((tm, tn), lambda i, j: (i, j)),
            pl.BlockSpec((tm, kr), lambda i, j: (i, 0)),
            pl.BlockSpec((tm, kp), lambda i, j: (i, 0)),
            pl.BlockSpec((kr, tn), lambda i, j: (0, j)),
            pl.BlockSpec((kp, tn), lambda i, j: (1, j)),
        ],
        out_specs=pl.BlockSpec((tm, tn), lambda i, j: (i, j)),
        out_shape=jax.ShapeDtypeStruct((s, d), _F32),
        compiler_params=_params("parallel", "arbitrary"),
        name="proj_out",
    )(x, ret, pool, w, w)


def kernel(x, positions, ffn1_norm, ffn1_w_gate, ffn1_w_up, ffn1_w_down, mix_norm, w_in,
           ret_decay_fwd, ret_decay_bwd, ret_head_norm, pool_w, pool_scale, w_out,
           ffn2_norm, ffn2_w_gate, ffn2_w_up, ffn2_w_down, final_norm):
    b, s, d = x.shape
    assert b == 1, "retention scan is written for a single sequence"
    depth = ffn1_norm.shape[0]
    heads = ret_decay_fwd.shape[1]
    ret_dim = ret_head_norm.shape[1]
    head_dim = ret_dim // heads

    freqs = 1.0 / (ROPE_BASE ** (jnp.arange(0, head_dim, 2, dtype=_F32) / head_dim))
    freqs = freqs.reshape(1, head_dim // 2)
    pos = positions.reshape(s, 1)
    final_gain = final_norm.reshape(1, d)
    row = lambda v: v.reshape(1, -1).astype(_F32)
    bf = lambda w: w.astype(_BF16)

    xs = x.reshape(s, d)
    for layer in range(depth):
        last = layer == depth - 1
        xs, w_in_bf, w_out_bf = _ffn(
            xs, row(ffn1_norm[layer]), ffn1_w_gate[layer], ffn1_w_up[layer], ffn1_w_down[layer],
            final_gain, final_norm=False, casts=(w_in[layer], w_out[layer]))
        zr, zf = _proj_in(xs, row(mix_norm[layer]), pos, freqs, w_in_bf,
                          ret_dim=ret_dim, head_dim=head_dim)
        ret = _retention(zr, zf, ret_decay_fwd[layer].reshape(heads, 1, 1).astype(_F32),
                         ret_decay_bwd[layer].reshape(heads, 1, 1).astype(_F32),
                         row(ret_head_norm[layer]), heads=heads, head_dim=head_dim)
        pool = _pool(zf, bf(pool_w[layer]), row(pool_scale[layer]))
        xs = _proj_out(xs, ret, pool, w_out_bf)
        xs, = _ffn(xs, row(ffn2_norm[layer]), ffn2_w_gate[layer], ffn2_w_up[layer], ffn2_w_down[layer],
                   final_gain, final_norm=last)
    return xs.reshape(b, s, d)
```

```python
import functools

import jax
import jax.numpy as jnp
from jax import lax
from jax.experimental import pallas as pl
from jax.experimental.pallas import tpu as pltpu

EPS = 1e-6
ROPE_BASE = 10000.0
POOL_WINDOWS = (2, 4, 8, 16)
POOL_HALO = 16

_F32 = jnp.float32
_BF16 = jnp.bfloat16
_VMEM_LIMIT_BYTES = 62 * 1024 * 1024


def _params(*semantics):
    return pltpu.CompilerParams(dimension_semantics=semantics, vmem_limit_bytes=_VMEM_LIMIT_BYTES)


def _rmsnorm(x, gain):
    return x * lax.rsqrt(jnp.mean(x * x, axis=-1, keepdims=True) + EPS) * gain


def _silu(g):
    return g * jax.nn.sigmoid(g)


def _ffn_head_kernel(x_hbm, gain_ref, wg_ref, wu_ref, wd_ref, o_hbm, wg_hbm, wu_hbm, wd_hbm,
                     acc_ref, h_ref, wg_bf, wu_bf, wd_bf, sem, w_sem,
                     *, n_steps, tf, row_chunk, row_part, col_chunk):
    j = pl.program_id(0)
    tm, d = acc_ref.shape

    def weight_copies(step):
        cols = pl.ds(pl.multiple_of(step * tf, tf), tf)
        return (pltpu.make_async_copy(wg_bf, wg_hbm.at[:, cols], w_sem.at[0]),
                pltpu.make_async_copy(wu_bf, wu_hbm.at[:, cols], w_sem.at[1]),
                pltpu.make_async_copy(wd_bf, wd_hbm.at[cols, :], w_sem.at[2]))

    @pl.when(j == 0)
    def _():
        copy = pltpu.make_async_copy(x_hbm.at[pl.ds(0, tm), :], acc_ref, sem)
        copy.start()
        copy.wait()
        for r in range(0, tm, row_chunk):
            rows = slice(r, r + row_chunk)
            h_ref[rows, :] = _rmsnorm(acc_ref[rows, :], gain_ref[...]).astype(_BF16)

    @pl.when(j > 0)
    def _():
        for copy in weight_copies(j - 1):
            copy.wait()

    wg_bf[...] = wg_ref[...].astype(_BF16)
    wu_bf[...] = wu_ref[...].astype(_BF16)
    wd_bf[...] = wd_ref[...].astype(_BF16)
    for copy in weight_copies(j):
        copy.start()

    parts = [slice(r, r + row_part) for r in range(0, tm, row_part)]
    acts = []
    for rows in parts:
        h = h_ref[rows, :]
        g = jnp.dot(h, wg_bf[...], preferred_element_type=_F32)
        u = jnp.dot(h, wu_bf[...], preferred_element_type=_F32)
        acts.append((_silu(g) * u * 0.5).astype(_BF16))
    for rows, a in zip(parts, acts):
        for c in range(0, d, col_chunk):
            cols = slice(c, c + col_chunk)
            acc_ref[rows, cols] += jnp.dot(a, wd_bf[:, cols], preferred_element_type=_F32)

    @pl.when(j == n_steps - 1)
    def _():
        for copy in weight_copies(j):
            copy.wait()
        copy = pltpu.make_async_copy(acc_ref, o_hbm, sem)
        copy.start()
        copy.wait()


def _ffn_head(x, gain, wg, wu, wd, *, rows, tf):
    s, d = x.shape
    f = wg.shape[1]
    assert f % tf == 0 and rows % 256 == 0
    n_steps = f // tf
    any_spec = pl.BlockSpec(memory_space=pl.ANY)
    return pl.pallas_call(
        functools.partial(_ffn_head_kernel, n_steps=n_steps, tf=tf, row_chunk=128, row_part=256,
                          col_chunk=1024),
        grid=(n_steps,),
        in_specs=[
            any_spec,
            pl.BlockSpec((1, d), lambda j: (0, 0)),
            pl.BlockSpec((d, tf), lambda j: (0, j)),
            pl.BlockSpec((d, tf), lambda j: (0, j)),
            pl.BlockSpec((tf, d), lambda j: (j, 0)),
        ],
        out_specs=[any_spec, any_spec, any_spec, any_spec],
        out_shape=[jax.ShapeDtypeStruct((rows, d), _F32), jax.ShapeDtypeStruct(wg.shape, _BF16),
                   jax.ShapeDtypeStruct(wu.shape, _BF16), jax.ShapeDtypeStruct(wd.shape, _BF16)],
        scratch_shapes=[pltpu.VMEM((rows, d), _F32), pltpu.VMEM((rows, d), _BF16),
                        pltpu.VMEM((d, tf), _BF16), pltpu.VMEM((d, tf), _BF16), pltpu.VMEM((tf, d), _BF16),
                        pltpu.SemaphoreType.DMA, pltpu.SemaphoreType.DMA((3,))],
        compiler_params=_params("arbitrary"),
        name="ffn_head",
    )(x, gain, wg, wu, wd)


def _ffn_kernel(x_hbm, gain_ref, wg_ref, wu_ref, wd_ref, fgain_ref, *rest, head_blocks,
                n_casts, n_steps, n_row_blocks, tf, tail_subs, final_norm, row_chunk, col_chunk):
    if head_blocks:
        head_hbm, rest = rest[0], rest[1:]
    cast_in = rest[:n_casts]
    o_ref = rest[n_casts]
    cast_out = rest[n_casts + 1:2 * n_casts + 1]
    x_buf, h_ref, x_sem = rest[2 * n_casts + 1:]
    i = pl.program_id(0)
    j = pl.program_id(1)
    tm, d = o_ref.shape
    computed = i >= head_blocks

    def x_copy(block):
        return pltpu.make_async_copy(x_hbm.at[pl.ds(block * tm, tm), :], x_buf, x_sem)

    def activation(k):
        cols = slice(k * tf, (k + 1) * tf)
        h = h_ref[...]
        g = jnp.dot(h, wg_ref[:, cols], preferred_element_type=_F32)
        u = jnp.dot(h, wu_ref[:, cols], preferred_element_type=_F32)
        return (_silu(g) * u * 0.5).astype(_BF16)

    def cast_blocks():
        for src, dst in zip(cast_in, cast_out):
            dst[...] = src[...].astype(dst.dtype)

    def step(n_subs):
        acts = [activation(k) for k in range(n_subs)]
        cast_blocks()
        for k, a in enumerate(acts):
            for c in range(0, d, col_chunk):
                cols = slice(c, c + col_chunk)
                o_ref[:, cols] += jnp.dot(a, wd_ref[k * tf:(k + 1) * tf, cols],
                                          preferred_element_type=_F32)

    if head_blocks:
        @pl.when(i < head_blocks)
        def _():
            @pl.when(j == 0)
            def _():
                copy = pltpu.make_async_copy(head_hbm.at[pl.ds(i * tm, tm), :], x_buf, x_sem)
                copy.start()
                copy.wait()
                for r in range(0, tm, row_chunk):
                    rows = slice(r, r + row_chunk)
                    y = x_buf[rows, :]
                    o_ref[rows, :] = _rmsnorm(y, fgain_ref[...]) if final_norm else y

            cast_blocks()

    @pl.when(jnp.logical_and(j == 0, computed))
    def _():
        if not head_blocks:
            @pl.when(i == 0)
            def _():
                x_copy(0).start()

        x_copy(i).wait()
        for r in range(0, tm, row_chunk):
            rows = slice(r, r + row_chunk)
            xr = x_buf[rows, :]
            h_ref[rows, :] = _rmsnorm(xr, gain_ref[...]).astype(_BF16)
            o_ref[rows, :] = xr
        step(2)

    @pl.when(j == 1)
    def _():
        @pl.when(jnp.logical_and(i + 1 < n_row_blocks, i + 1 >= head_blocks))
        def _():
            x_copy(i + 1).start()

    @pl.when(jnp.logical_and(jnp.logical_and(j > 0, j < n_steps - 1), computed))
    def _():
        step(2)

    @pl.when(jnp.logical_and(j == n_steps - 1, computed))
    def _():
        step(tail_subs)
        if final_norm:
            for r in range(0, tm, row_chunk):
                rows = slice(r, r + row_chunk)
                o_ref[rows, :] = _rmsnorm(o_ref[rows, :], fgain_ref[...])


_CAST_ROWS = 16


def _cast_specs(w, n_steps, total_steps):
    rows, width = w.shape
    assert rows % (2 * _CAST_ROWS) == 0
    shapes = [(r, k) for r in (_CAST_ROWS, 2 * _CAST_ROWS) for k in (4, 2, 1)
              if width % (k * 128) == 0 and (rows // r) * k <= total_steps]
    if not shapes:
        return None
    r, k = shapes[0]
    n_blocks = (rows // r) * k

    def index(i, j):
        b = jnp.minimum(i * n_steps + j, n_blocks - 1)
        return b // k, b % k

    block = (r, width // k)
    return pl.BlockSpec(block, index), pl.BlockSpec(block, index), jax.ShapeDtypeStruct(w.shape, _BF16)


def _ffn(x, gain, wg, wu, wd, fgain, *, final_norm, casts=(), tm=512, tf=256):
    s, d = x.shape
    f = wg.shape[1]
    assert s % tm == 0 and f % tf == 0
    n_steps = pl.cdiv(f, 2 * tf)
    assert n_steps >= 3
    tail_subs = (f - (n_steps - 1) * 2 * tf) // tf
    n_row_blocks = s // tm
    head_blocks = min(2, n_row_blocks - 1) if wg.dtype != _BF16 else 0
    head = ()
    if head_blocks:
        head_rows, wg, wu, wd = _ffn_head(x, gain, wg, wu, wd, rows=head_blocks * tm, tf=tf)
        head = (head_rows,)
    w_col = lambda i, j: jnp.where(i < head_blocks, 0, j)
    all_specs = [_cast_specs(w, n_steps, n_row_blocks * n_steps) for w in casts]
    hosted = [w for w, spec in zip(casts, all_specs) if spec is not None]
    cast_specs = [spec for spec in all_specs if spec is not None]
    kern = functools.partial(
        _ffn_kernel, head_blocks=head_blocks, n_casts=len(hosted), n_steps=n_steps, n_row_blocks=n_row_blocks,
        tf=tf, tail_subs=tail_subs, final_norm=final_norm, row_chunk=128, col_chunk=1024)
    out, *hosted_bf = pl.pallas_call(
        kern,
        grid=(n_row_blocks, n_steps),
        in_specs=[
            pl.BlockSpec(memory_space=pl.ANY),
            pl.BlockSpec((1, d), lambda i, j: (0, 0)),
            pl.BlockSpec((d, 2 * tf), lambda i, j: (0, w_col(i, j))),
            pl.BlockSpec((d, 2 * tf), lambda i, j: (0, w_col(i, j))),
            pl.BlockSpec((2 * tf, d), lambda i, j: (w_col(i, j), 0)),
            pl.BlockSpec((1, d), lambda i, j: (0, 0)),
        ] + [pl.BlockSpec(memory_space=pl.ANY)] * len(head) + [spec[0] for spec in cast_specs],
        out_specs=[pl.BlockSpec((tm, d), lambda i, j: (i, 0))] + [spec[1] for spec in cast_specs],
        out_shape=[jax.ShapeDtypeStruct((s, d), _F32)] + [spec[2] for spec in cast_specs],
        scratch_shapes=[pltpu.VMEM((tm, d), _F32), pltpu.VMEM((tm, d), _BF16),
                        pltpu.SemaphoreType.DMA],
        compiler_params=_params("arbitrary", "arbitrary"),
        name="ffn",
    )(x, gain, wg.astype(_BF16), wu.astype(_BF16), wd.astype(_BF16), fgain, *head, *hosted)
    hosted_bf = iter(hosted_bf)
    return [out] + [next(hosted_bf) if spec is not None else w.astype(_BF16)
                    for w, spec in zip(casts, all_specs)]


def _proj_in_kernel(x_hbm, gain_ref, pos_ref, freq_ref, w_ref, cast_ref, zr_ref, zf_ref, cast_out,
                    x_buf, h_ref, cos_ref, sin_ref, x_sem,
                    *, n_row_blocks, n_plain, n_q_blocks, heads_per_block, head_dim, k_scale, row_chunk):
    i = pl.program_id(0)
    j = pl.program_id(1)
    tm = x_buf.shape[0]
    half = head_dim // 2

    def x_copy(block):
        return pltpu.make_async_copy(x_hbm.at[pl.ds(block * tm, tm), :], x_buf, x_sem)

    def head_dot(t):
        cols = slice(t * head_dim, (t + 1) * head_dim)
        return jnp.dot(h_ref[...], w_ref[:, cols], preferred_element_type=_F32)

    def plain_step():
        cast_out[...] = cast_ref[...].astype(cast_out.dtype)
        for t in range(heads_per_block):
            zf_ref[:, t * head_dim:(t + 1) * head_dim] = head_dot(t)

    @pl.when(j == 0)
    def _():
        @pl.when(i == 0)
        def _():
            x_copy(0).start()

        x_copy(i).wait()
        for r in range(0, tm, row_chunk):
            rows = slice(r, r + row_chunk)
            h_ref[rows, :] = _rmsnorm(x_buf[rows, :], gain_ref[...]).astype(_BF16)
        ang = pos_ref[...].astype(_F32) * freq_ref[...]
        cos_ref[...] = jnp.cos(ang)
        sin_ref[...] = jnp.sin(ang)
        plain_step()

    @pl.when(j == 1)
    def _():
        @pl.when(i + 1 < n_row_blocks)
        def _():
            x_copy(i + 1).start()

    @pl.when(jnp.logical_and(j > 0, j < n_plain))
    def _():
        plain_step()

    @pl.when(j >= n_plain)
    def _():
        cast_out[...] = cast_ref[...].astype(cast_out.dtype)
        cos = cos_ref[...]
        sin = sin_ref[...]
        scale = jnp.where(j - n_plain >= n_q_blocks, k_scale, 1.0).astype(_F32)
        for t in range(heads_per_block):
            z = head_dot(t)
            t1 = z[:, :half]
            t2 = z[:, half:]
            lo = t * head_dim
            zr_ref[:, lo:lo + half] = ((t1 * cos - t2 * sin) * scale).astype(zr_ref.dtype)
            zr_ref[:, lo + half:lo + head_dim] = ((t1 * sin + t2 * cos) * scale).astype(zr_ref.dtype)


def _proj_in(x, gain, pos, freqs, w, cast, *, ret_dim, head_dim, tm=1024, tn=1024):
    s, d = x.shape
    n = w.shape[1]
    assert s % tm == 0 and n % tn == 0 and ret_dim % tn == 0 and tn % head_dim == 0
    half = head_dim // 2
    n_rot = 2 * ret_dim // tn
    n_plain = n // tn - n_rot
    assert n_plain >= 2
    n_col = n // tn
    spec = _cast_specs(cast, n_col, (s // tm) * n_col)
    if spec is None:
        zr, zf, _ = _proj_in(x, gain, pos, freqs, w, jnp.zeros((2 * _CAST_ROWS, 128), _F32),
                             ret_dim=ret_dim, head_dim=head_dim, tm=tm, tn=tn)
        return zr, zf, cast.astype(_BF16)
    kern = functools.partial(
        _proj_in_kernel, n_row_blocks=s // tm, n_plain=n_plain, n_q_blocks=ret_dim // tn,
        heads_per_block=tn // head_dim, head_dim=head_dim, k_scale=head_dim ** -0.5, row_chunk=128)
    return pl.pallas_call(
        kern,
        grid=(s // tm, n // tn),
        in_specs=[
            pl.BlockSpec(memory_space=pl.ANY),
            pl.BlockSpec((1, d), lambda i, j: (0, 0)),
            pl.BlockSpec((tm, 1), lambda i, j: (i, 0)),
            pl.BlockSpec((1, half), lambda i, j: (0, 0)),
            pl.BlockSpec((d, tn), lambda i, j: (0, jnp.where(j < n_plain, j + n_rot, j - n_plain))),
            spec[0],
        ],
        out_specs=[
            pl.BlockSpec((tm, tn), lambda i, j: (i, jnp.maximum(j - n_plain, 0))),
            pl.BlockSpec((tm, tn), lambda i, j: (i, jnp.minimum(j, n_plain - 1))),
            spec[1],
        ],
        out_shape=[jax.ShapeDtypeStruct((s, n_rot * tn), _BF16),
                   jax.ShapeDtypeStruct((s, n_plain * tn), _F32), spec[2]],
        scratch_shapes=[pltpu.VMEM((tm, d), _F32), pltpu.VMEM((tm, d), _BF16),
                        pltpu.VMEM((tm, half), _F32), pltpu.VMEM((tm, half), _F32),
                        pltpu.SemaphoreType.DMA],
        compiler_params=_params("arbitrary", "arbitrary"),
        name="proj_in",
    )(x, gain, pos, freqs, w, cast)


def _contract_rows(a, b):
    return lax.dot_general(a, b, (((0,), (0,)), ((), ())), preferred_element_type=_F32)


def _ret_state_kernel(lgb_ref, k_ref, v_ref, b_ref, state_ref, vdec_ref, cdec_ref, *, chunk, heads, hd):
    t = pl.program_id(0)

    @pl.when(t == 0)
    def _():
        state_ref[...] = jnp.zeros_like(state_ref)
        idx = lax.broadcasted_iota(jnp.int32, (chunk, hd), 0).astype(_F32)
        for h in range(heads):
            lgb = jax.nn.log_sigmoid(lgb_ref[h])
            vdec_ref[h] = jnp.exp(idx * lgb)
            cdec_ref[h] = jnp.broadcast_to(jnp.exp(chunk * lgb), (1, hd))

    for h in range(heads):
        cols = slice(h * hd, (h + 1) * hd)
        b_ref[h, 0] = state_ref[h].astype(_BF16)
        v_dec = (v_ref[:, cols] * vdec_ref[h]).astype(_BF16)
        state_ref[h] = state_ref[h] * cdec_ref[h] + _contract_rows(k_ref[:, cols], v_dec)


def _ret_out_kernel(lgf_ref, lgb_ref, qk_ref, vg_ref, b_ref, hn_ref, o_ref,
                    state_ref, dm_ref, dec_ref, cdec_ref, *, chunk, heads, hd):
    c = pl.program_id(0)
    rd = heads * hd

    @pl.when(c == 0)
    def _():
        state_ref[...] = jnp.zeros_like(state_ref)
        rel = (lax.broadcasted_iota(jnp.int32, (chunk, chunk), 0)
               - lax.broadcasted_iota(jnp.int32, (chunk, chunk), 1)).astype(_F32)
        idx = lax.broadcasted_iota(jnp.int32, (chunk, hd), 0).astype(_F32)
        for h in range(heads):
            lgf = jax.nn.log_sigmoid(lgf_ref[h])
            lgb = jax.nn.log_sigmoid(lgb_ref[h])
            dm_ref[h] = jnp.exp(jnp.where(rel >= 0, rel * lgf, -rel * lgb))
            dec_ref[0, h] = jnp.exp((idx + 1.0) * lgf)
            dec_ref[1, h] = jnp.exp((chunk - idx) * lgb)
            dec_ref[2, h] = jnp.exp((chunk - 1.0 - idx) * lgf)
            cdec_ref[h] = jnp.broadcast_to(jnp.exp(chunk * lgf), (1, hd))

    for h in range(heads):
        cols = slice(h * hd, (h + 1) * hd)
        second_cols = slice(rd + h * hd, rd + (h + 1) * hd)
        q = qk_ref[:, cols]
        k = qk_ref[:, second_cols]
        v = vg_ref[:, cols]
        scores = lax.dot_general(q, k, (((1,), (1,)), ((), ())), preferred_element_type=_F32)
        scores = scores * dm_ref[h]
        out = jnp.dot(scores.astype(_BF16), v.astype(_BF16), preferred_element_type=_F32)
        out += jnp.dot(q, state_ref[h].astype(_BF16), preferred_element_type=_F32) * dec_ref[0, h]
        out += jnp.dot(q, b_ref[h, 0], preferred_element_type=_F32) * dec_ref[1, h]

        out = out * lax.rsqrt(jnp.mean(out * out, axis=-1, keepdims=True) + EPS)
        out = out * hn_ref[:, cols]
        o_ref[:, cols] = (out * _silu(vg_ref[:, second_cols])).astype(o_ref.dtype)

        v_dec = (v * dec_ref[2, h]).astype(_BF16)
        state_ref[h] = state_ref[h] * cdec_ref[h] + _contract_rows(k, v_dec)


def _retention(zr, zf, logit_f, logit_b, head_gain, *, heads, head_dim, chunk=512):
    s = zr.shape[0]
    assert s % chunk == 0
    nc = s // chunk
    hd = head_dim
    rd = heads * hd
    assert zr.shape[1] == 2 * rd and zf.shape[1] >= 2 * rd
    decay_spec = pl.BlockSpec((heads, 1, 1), lambda c: (0, 0, 0))

    b_states = pl.pallas_call(
        functools.partial(_ret_state_kernel, chunk=chunk, heads=heads, hd=hd),
        grid=(nc,),
        in_specs=[
            decay_spec,
            pl.BlockSpec((chunk, rd), lambda t: (nc - 1 - t, 1)),
            pl.BlockSpec((chunk, rd), lambda t: (nc - 1 - t, 0)),
        ],
        out_specs=pl.BlockSpec((heads, 1, hd, hd), lambda t: (0, nc - 1 - t, 0, 0)),
        out_shape=jax.ShapeDtypeStruct((heads, nc, hd, hd), _BF16),
        scratch_shapes=[pltpu.VMEM((heads, hd, hd), _F32), pltpu.VMEM((heads, chunk, hd), _F32),
                        pltpu.VMEM((heads, 1, hd), _F32)],
        compiler_params=_params("arbitrary"),
        name="ret_state",
    )(logit_b, zr, zf)

    return pl.pallas_call(
        functools.partial(_ret_out_kernel, chunk=chunk, heads=heads, hd=hd),
        grid=(nc,),
        in_specs=[
            decay_spec,
            decay_spec,
            pl.BlockSpec((chunk, 2 * rd), lambda c: (c, 0)),
            pl.BlockSpec((chunk, 2 * rd), lambda c: (c, 0)),
            pl.BlockSpec((heads, 1, hd, hd), lambda c: (0, c, 0, 0)),
            pl.BlockSpec((1, rd), lambda c: (0, 0)),
        ],
        out_specs=pl.BlockSpec((chunk, rd), lambda c: (c, 0)),
        out_shape=jax.ShapeDtypeStruct((s, rd), _BF16),
        scratch_shapes=[pltpu.VMEM((heads, hd, hd), _F32), pltpu.VMEM((heads, chunk, chunk), _F32),
                        pltpu.VMEM((3, heads, chunk, hd), _F32), pltpu.VMEM((heads, 1, hd), _F32)],
        compiler_params=_params("arbitrary"),
        name="ret_out",
    )(logit_f, logit_b, zr, zf, b_states, head_gain)


def _window_sum(p, window):
    rows = p.shape[0]
    assert window >= 2 and window & (window - 1) == 0
    shifted = lambda a, k: pltpu.roll(a, k % rows, axis=0)
    s = shifted(p, 1) + p
    width = 2
    while width < window:
        s = shifted(s, width // 2) + shifted(s, -(width // 2))
        width *= 2
    return s


def _pool_kernel(prev_ref, cur_ref, next_ref, w_ref, scale_ref, o_ref, pad_ref, *, tm, seq, group_dim):
    i = pl.program_id(0)
    n_blocks = pl.num_programs(0)
    halo = POOL_HALO
    pad_ref[0:halo, :] = jnp.where(i > 0, prev_ref[...], 0.0)
    pad_ref[halo:halo + tm, :] = cur_ref[...]
    pad_ref[halo + tm:, :] = jnp.where(i < n_blocks - 1, next_ref[...], 0.0)

    row = i * tm + lax.broadcasted_iota(jnp.int32, (tm, 1), 0)
    for gi, window in enumerate(POOL_WINDOWS):
        lo = window // 2
        hi = window - 1 - lo
        cols = slice(gi * group_dim, (gi + 1) * group_dim)
        total = _window_sum(pad_ref[:, cols], window)[halo:halo + tm]
        count = (jnp.minimum(row + hi + 1, seq) - jnp.maximum(row - lo, 0)).astype(_F32)
        pooled = total / count - cur_ref[:, cols]
        mapped = jnp.dot(pooled.astype(_BF16), w_ref[gi], preferred_element_type=_F32)
        o_ref[:, cols] = (mapped * scale_ref[:, cols]).astype(o_ref.dtype)


def _pool(z, pool_w, pool_scale, *, tm=1024):
    s, n = z.shape
    groups, group_dim, _ = pool_w.shape
    pool_dim = groups * group_dim
    assert groups == len(POOL_WINDOWS) and max(POOL_WINDOWS) // 2 <= POOL_HALO
    assert s % tm == 0 and tm % POOL_HALO == 0 and n % pool_dim == 0
    col = n // pool_dim - 1
    r = tm // POOL_HALO
    last = s // POOL_HALO - 1
    return pl.pallas_call(
        functools.partial(_pool_kernel, tm=tm, seq=s, group_dim=group_dim),
        grid=(s // tm,),
        in_specs=[
            pl.BlockSpec((POOL_HALO, pool_dim), lambda i: (jnp.maximum(i * r - 1, 0), col)),
            pl.BlockSpec((tm, pool_dim), lambda i: (i, col)),
            pl.BlockSpec((POOL_HALO, pool_dim), lambda i: (jnp.minimum((i + 1) * r, last), col)),
            pl.BlockSpec((groups, group_dim, group_dim), lambda i: (0, 0, 0)),
            pl.BlockSpec((1, pool_dim), lambda i: (0, 0)),
        ],
        out_specs=pl.BlockSpec((tm, pool_dim), lambda i: (i, 0)),
        out_shape=jax.ShapeDtypeStruct((s, pool_dim), _BF16),
        scratch_shapes=[pltpu.VMEM((tm + 2 * POOL_HALO, pool_dim), _F32)],
        compiler_params=_params("parallel"),
        name="pool",
    )(z, z, z, pool_w, pool_scale)


def _proj_out_kernel(x_ref, ret_ref, pool_ref, wr_ref, wp_ref, o_ref):
    acc = jnp.dot(ret_ref[...], wr_ref[...], preferred_element_type=_F32)
    acc += jnp.dot(pool_ref[...], wp_ref[...], preferred_element_type=_F32)
    o_ref[...] = x_ref[...] + acc


def _proj_out(x, ret, pool, w, *, tm=1024, tn=1024):
    s, d = x.shape
    kr = ret.shape[1]
    kp = pool.shape[1]
    assert kr == kp and w.shape == (kr + kp, d) and s % tm == 0 and d % tn == 0
    return pl.pallas_call(
        _proj_out_kernel,
        grid=(s // tm, d // tn),
        in_specs=[
            pl.BlockSpec((tm, tn), lambda i, j: (i, j)),
            pl.BlockSpec((tm, kr), lambda i, j: (i, 0)),
            pl.BlockSpec((tm, kp), lambda i, j: (i, 0)),
            pl.BlockSpec((kr, tn), lambda i, j: (0, j)),
            pl.BlockSpec((kp, tn), lambda i, j: (1, j)),
        ],
        out_specs=pl.BlockSpec((tm, tn), lambda i, j: (i, j)),
        out_shape=jax.ShapeDtypeStruct((s, d), _F32),
        compiler_params=_params("parallel", "arbitrary"),
        name="proj_out",
    )(x, ret, pool, w, w)


def kernel(x, positions, ffn1_norm, ffn1_w_gate, ffn1_w_up, ffn1_w_down, mix_norm, w_in,
           ret_decay_fwd, ret_decay_bwd, ret_head_norm, pool_w, pool_scale, w_out,
           ffn2_norm, ffn2_w_gate, ffn2_w_up, ffn2_w_down, final_norm):
    b, s, d = x.shape
    assert b == 1, "retention scan is written for a single sequence"
    depth = ffn1_norm.shape[0]
    heads = ret_decay_fwd.shape[1]
    ret_dim = ret_head_norm.shape[1]
    head_dim = ret_dim // heads

    freqs = 1.0 / (ROPE_BASE ** (jnp.arange(0, head_dim, 2, dtype=_F32) / head_dim))
    freqs = freqs.reshape(1, head_dim // 2)
    pos = positions.reshape(s, 1)
    final_gain = final_norm.reshape(1, d)
    row = lambda v: v.reshape(1, -1).astype(_F32)
    bf = lambda w: w.astype(_BF16)

    xs = x.reshape(s, d)
    for layer in range(depth):
        last = layer == depth - 1
        xs, w_in_bf = _ffn(
            xs, row(ffn1_norm[layer]), ffn1_w_gate[layer], ffn1_w_up[layer], ffn1_w_down[layer],
            final_gain, final_norm=False, casts=(w_in[layer],))
        zr, zf, w_out_bf = _proj_in(xs, row(mix_norm[layer]), pos, freqs, w_in_bf, w_out[layer],
                                    ret_dim=ret_dim, head_dim=head_dim)
        ret = _retention(zr, zf, ret_decay_fwd[layer].reshape(heads, 1, 1).astype(_F32),
                         ret_decay_bwd[layer].reshape(heads, 1, 1).astype(_F32),
                         row(ret_head_norm[layer]), heads=heads, head_dim=head_dim)
        pool = _pool(zf, bf(pool_w[layer]), row(pool_scale[layer]))
        xs = _proj_out(xs, ret, pool, w_out_bf)
        xs, = _ffn(xs, row(ffn2_norm[layer]), ffn2_w_gate[layer], ffn2_w_up[layer], ffn2_w_down[layer],
                   final_gain, final_norm=last)
    return xs.reshape(b, s, d)
```
